```python
import math
import jax, jax.numpy as jnp
from jax import lax
import numpy as np

D_MODEL = 1024
BATCH = 32
SEQ = 2048
DEPTH = 4

N_A = DEPTH // 2
N_B = DEPTH - N_A

D_FF = ((8 * D_MODEL // 3) + 255) // 256 * 256

RET_DK = 256
RET_HEADS = D_MODEL // RET_DK
RET_DV = 2 * D_MODEL // RET_HEADS
RET_CHUNK = 128
RET_ROT_BASE = 10000.0

ATT_HD = 64
ATT_QH = D_MODEL // ATT_HD
ATT_KVH = ATT_QH // 8
ATT_G = ATT_QH // ATT_KVH
WINDOW = 128
ROPE_THETA = 10000.0

NORM_EPS = 1e-6
NEG_INF = -1e30

kernel_name = "yoco_retention_swa_sink_macaron"


def rms_norm(x, g):
    xf = x.astype(jnp.float32)
    y = xf * lax.rsqrt(jnp.mean(xf * xf, axis=-1, keepdims=True) + NORM_EPS)
    return (y * g.astype(jnp.float32)).astype(x.dtype)


def swiglu(h, w_in, w_out):
    gate, up = jnp.split(h @ w_in, 2, axis=-1)
    return (jax.nn.silu(gate) * up) @ w_out


def rope_half(x):
    s, d = x.shape[1], x.shape[-1]
    inv_freq = 1.0 / (ROPE_THETA ** (jnp.arange(0, d, 2, dtype=jnp.float32) / d))
    ang = jnp.arange(s, dtype=jnp.float32)[:, None] * inv_freq[None, :]
    cos, sin = jnp.cos(ang)[None, :, None, :], jnp.sin(ang)[None, :, None, :]
    xf = x.astype(jnp.float32)
    x1, x2 = xf[..., : d // 2], xf[..., d // 2:]
    return jnp.concatenate([x1 * cos - x2 * sin, x2 * cos + x1 * sin], axis=-1).astype(x.dtype)


def retnet_rotate(x):
    b, s, h, d = x.shape
    freq = 1.0 / (RET_ROT_BASE ** jnp.linspace(0.0, 1.0, d // 2, dtype=jnp.float32))
    ang = jnp.arange(s, dtype=jnp.float32)[:, None] * freq[None, :]
    cos, sin = jnp.cos(ang)[None, :, None, :], jnp.sin(ang)[None, :, None, :]
    xp = x.reshape(b, s, h, d // 2, 2)
    xe, xo = xp[..., 0], xp[..., 1]
    return jnp.stack([xe * cos - xo * sin, xo * cos + xe * sin], axis=-1).reshape(b, s, h, d)


def retention(h, w_in, w_out):
    b, s, _ = h.shape
    nc = s // RET_CHUNK
    dq = RET_HEADS * RET_DK
    dv = RET_HEADS * RET_DV
    q, k, v, g = jnp.split(h @ w_in, [dq, 2 * dq, 2 * dq + dv], axis=-1)
    q = retnet_rotate(q.astype(jnp.float32).reshape(b, s, RET_HEADS, RET_DK))
    k = retnet_rotate(k.astype(jnp.float32).reshape(b, s, RET_HEADS, RET_DK)) * (RET_DK ** -0.5)
    v = v.astype(jnp.float32).reshape(b, s, RET_HEADS, RET_DV)

    log_gamma = jnp.log(1.0 - 2.0 ** (-5.0 - jnp.arange(RET_HEADS, dtype=jnp.float32)))
    n = jnp.arange(RET_CHUNK, dtype=jnp.float32)
    diff = n[:, None] - n[None, :]
    decay_mask = jnp.where(diff[None] >= 0,
                           jnp.exp(jnp.maximum(diff, 0.0)[None] * log_gamma[:, None, None]), 0.0)
    xi = jnp.exp((n[:, None] + 1.0) * log_gamma[None, :])[:, :, None]
    zeta = jnp.exp((RET_CHUNK - 1.0 - n[:, None]) * log_gamma[None, :])[:, :, None]
    chunk_decay = jnp.exp(RET_CHUNK * log_gamma)[:, None, None]

    def to_chunks(t):
        return jnp.moveaxis(t.reshape(b, nc, RET_CHUNK, RET_HEADS, t.shape[-1]), 1, 0)

    def step(state, qkv):
        qc, kc, vc = qkv
        scores = jnp.einsum('bchd,bmhd->bhcm', qc, kc) * decay_mask
        inner = jnp.einsum('bhcm,bmhe->bche', scores, vc)
        cross = jnp.einsum('bchd,bhde->bche', qc, state) * xi
        state = state * chunk_decay + jnp.einsum('bmhd,bmhe->bhde', kc * zeta, vc)
        return state, inner + cross

    state0 = jnp.zeros((b, RET_HEADS, RET_DK, RET_DV), jnp.float32)
    _, out = lax.scan(step, state0, (to_chunks(q), to_chunks(k), to_chunks(v)))
    out = jnp.moveaxis(out, 0, 1).reshape(b, s, RET_HEADS, RET_DV)
    out = out * lax.rsqrt(jnp.mean(out * out, axis=-1, keepdims=True) + NORM_EPS)
    out = out.reshape(b, s, dv).astype(h.dtype) * jax.nn.silu(g)
    return out @ w_out


def shared_kv(x, kv_norm, kv_w, k_norm):
    b, s, _ = x.shape
    k, v = jnp.split(rms_norm(x, kv_norm) @ kv_w, 2, axis=-1)
    k = rope_half(rms_norm(k.reshape(b, s, ATT_KVH, ATT_HD), k_norm))
    v = v.reshape(b, s, ATT_KVH, ATT_HD)
    return k, v


def window_attention(h, k, v, w_q, q_norm, sinks, w_o):
    b, s, _ = h.shape
    nb = s // WINDOW
    q = rope_half(rms_norm((h @ w_q).reshape(b, s, ATT_QH, ATT_HD), q_norm))
    qb = jnp.moveaxis(q.reshape(b, nb, WINDOW, ATT_KVH, ATT_G, ATT_HD), 1, 0)

    def band(t):
        prev = jnp.pad(t, ((0, 0), (WINDOW, 0), (0, 0), (0, 0)))[:, :s]
        cur = t.reshape(b, nb, WINDOW, ATT_KVH, ATT_HD)
        prev = prev.reshape(b, nb, WINDOW, ATT_KVH, ATT_HD)
        return jnp.moveaxis(jnp.concatenate([prev, cur], axis=2), 1, 0)

    kb, vb = band(k), band(v)
    qi = jnp.arange(WINDOW)[:, None]
    kj = jnp.arange(2 * WINDOW)[None, :]
    rel = qi + WINDOW - kj
    sink_logits = jnp.broadcast_to(
        sinks.astype(jnp.float32).reshape(ATT_KVH, ATT_G)[None, :, :, None, None],
        (b, ATT_KVH, ATT_G, WINDOW, 1))
    scale = ATT_HD ** -0.5

    def block(args):
        qx, kx, vx, blk = args
        sc = jnp.einsum('bqkgd,bjkd->bkgqj', qx.astype(jnp.float32), kx.astype(jnp.float32)) * scale
        valid = (rel >= 0) & (rel < WINDOW) & (blk * WINDOW + kj - WINDOW >= 0)
        sc = jnp.where(valid[None, None, None], sc, NEG_INF)
        p = jax.nn.softmax(jnp.concatenate([sc, sink_logits], axis=-1), axis=-1)[..., :-1]
        return jnp.einsum('bkgqj,bjkd->bqkgd', p.astype(vx.dtype), vx)

    o = lax.map(block, (qb, kb, vb, jnp.arange(nb)))
    o = jnp.moveaxis(o, 0, 1).reshape(b, s, ATT_QH * ATT_HD)
    return o @ w_o


def setup_inputs(seed: int = 0) -> dict:
    key = jax.random.key(seed)
    ks = jax.random.split(key, 20)
    f32 = jnp.float32

    def w(k, shape, fan_in):
        return jax.random.normal(k, shape, f32) * (fan_in ** -0.5)

    def gain(k, shape):
        return 1.0 + 0.02 * jax.random.normal(k, shape, f32)

    return {
        "x": jax.random.normal(ks[0], (BATCH, SEQ, D_MODEL), f32),
        "ffn1_norm": gain(ks[1], (DEPTH, D_MODEL)),
        "ffn1_w_in": w(ks[2], (DEPTH, D_MODEL, 2 * D_FF), D_MODEL),
        "ffn1_w_out": w(ks[3], (DEPTH, D_FF, D_MODEL), D_FF),
        "mix_norm": gain(ks[4], (DEPTH, D_MODEL)),
        "ffn2_norm": gain(ks[5], (DEPTH, D_MODEL)),
        "ffn2_w_in": w(ks[6], (DEPTH, D_MODEL, 2 * D_FF), D_MODEL),
        "ffn2_w_out": w(ks[7], (DEPTH, D_FF, D_MODEL), D_FF),
        "ret_w_in": w(ks[8], (N_A, D_MODEL, 2 * RET_HEADS * RET_DK + 2 * RET_HEADS * RET_DV), D_MODEL),
        "ret_w_out": w(ks[9], (N_A, RET_HEADS * RET_DV, D_MODEL), RET_HEADS * RET_DV),
        "kv_norm": gain(ks[10], (D_MODEL,)),
        "kv_w": w(ks[11], (D_MODEL, 2 * ATT_KVH * ATT_HD), D_MODEL),
        "k_norm": gain(ks[12], (ATT_HD,)),
        "attn_w_q": w(ks[13], (N_B, D_MODEL, ATT_QH * ATT_HD), D_MODEL),
        "q_norm": gain(ks[14], (N_B, ATT_HD)),
        "attn_sinks": 0.1 * jax.random.normal(ks[15], (N_B, ATT_QH), f32),
        "attn_w_o": w(ks[16], (N_B, ATT_QH * ATT_HD, D_MODEL), ATT_QH * ATT_HD),
    }


def reference(x, ffn1_norm, ffn1_w_in, ffn1_w_out, mix_norm, ffn2_norm, ffn2_w_in, ffn2_w_out,
              ret_w_in, ret_w_out, kv_norm, kv_w, k_norm, attn_w_q, q_norm, attn_sinks, attn_w_o):
    k_sh, v_sh = None, None
    for i in range(DEPTH):
        x = x + 0.5 * swiglu(rms_norm(x, ffn1_norm[i]), ffn1_w_in[i], ffn1_w_out[i])
        hm = rms_norm(x, mix_norm[i])
        if i < N_A:
            x = x + retention(hm, ret_w_in[i], ret_w_out[i])
        else:
            j = i - N_A
            x = x + window_attention(hm, k_sh, v_sh, attn_w_q[j], q_norm[j], attn_sinks[j], attn_w_o[j])
        x = x + 0.5 * swiglu(rms_norm(x, ffn2_norm[i]), ffn2_w_in[i], ffn2_w_out[i])
        if i == N_A - 1:
            k_sh, v_sh = shared_kv(x, kv_norm, kv_w, k_norm)
    return x
```

```python
import functools

import jax
import jax.numpy as jnp
from jax import lax
from jax.experimental import pallas as pl
from jax.experimental.pallas import tpu as pltpu

F32 = jnp.float32
BF16 = jnp.bfloat16

NORM_EPS = 1e-6
MXU_TILE = 256
VMEM_LIMIT_BYTES = 56 * 1024 * 1024


def _resident(shape):
    return pl.BlockSpec(shape, lambda *_: (0,) * len(shape), pipeline_mode=pl.Buffered(1))


def _rms_scale(x):
    return lax.rsqrt(jnp.mean(x * x, axis=-1, keepdims=True) + NORM_EPS)


def _ffn_body(x_ref, g_ref, wg_ref, wu_ref, wo_ref, o_ref, h_ref, a_ref, *, ff_chunk):
    x = x_ref[...]
    h_ref[...] = (x * _rms_scale(x) * g_ref[...]).astype(BF16)
    d_ff = wg_ref.shape[1]
    for c in range(d_ff // ff_chunk):
        sl = slice(c * ff_chunk, (c + 1) * ff_chunk)
        gate = jnp.dot(h_ref[...], wg_ref[:, sl], preferred_element_type=F32)
        up = jnp.dot(h_ref[...], wu_ref[:, sl], preferred_element_type=F32)
        a_ref[:, sl] = (gate * jax.nn.sigmoid(gate) * up).astype(BF16)
    y = jnp.dot(a_ref[...], wo_ref[...], preferred_element_type=F32)
    o_ref[...] = x_ref[...] + 0.5 * y


def _ffn(x2d, g, wg, wu, wo, *, tm=512, ff_chunk=MXU_TILE):
    t, d = x2d.shape
    d_ff = wg.shape[1]
    assert t % tm == 0 and d_ff % ff_chunk == 0
    return pl.pallas_call(
        functools.partial(_ffn_body, ff_chunk=ff_chunk),
        grid=(t // tm,),
        in_specs=[
            pl.BlockSpec((tm, d), lambda i: (i, 0)),
            _resident((1, d)),
            _resident((d, d_ff)),
            _resident((d, d_ff)),
            _resident((d_ff, d)),
        ],
        out_specs=pl.BlockSpec((tm, d), lambda i: (i, 0)),
        out_shape=jax.ShapeDtypeStruct((t, d), F32),
        scratch_shapes=[pltpu.VMEM((tm, d), BF16), pltpu.VMEM((tm, d_ff), BF16)],
        compiler_params=pltpu.CompilerParams(
            dimension_semantics=("parallel",), vmem_limit_bytes=VMEM_LIMIT_BYTES),
        name="ffn",
    )(x2d, g.reshape(1, d), wg, wu, wo)


RET_DK = 256
RET_DV_FACTOR = 2
RET_ROT_BASE = 10000.0
NT_DIMS = (((1,), (1,)), ((), ()))
TN_DIMS = (((0,), (0,)), ((), ()))


def _ret_gamma(h):
    return 1.0 - 2.0 ** (-5.0 - h)


def _rotate_halves(t, cos, sin):
    half = t.shape[1] // 2
    te, to = t[:, :half], t[:, half:]
    return jnp.concatenate([te * cos - to * sin, to * cos + te * sin], axis=1)


def _ret_body(x_ref, g_ref, cq_ref, sq_ref, ck_ref, sk_ref, xi_ref, zeta_ref, dmask_ref, win_ref, wo_ref,
              o_ref, h_ref, gated_ref, state_ref, *, heads, dk, dv):
    chunk = x_ref.shape[1]

    @pl.when(pl.program_id(1) == 0)
    def _():
        state_ref[...] = jnp.zeros_like(state_ref)

    x = x_ref[0]
    h_ref[...] = (x * _rms_scale(x) * g_ref[...]).astype(BF16)
    k_off, v_off, g_off = heads * dk, 2 * heads * dk, 2 * heads * dk + heads * dv
    for hd in range(heads):
        proj = lambda off, width: jnp.dot(
            h_ref[...], win_ref[:, off + hd * width: off + (hd + 1) * width], preferred_element_type=F32)
        qr = _rotate_halves(proj(0, dk), cq_ref[...], sq_ref[...])
        kr = _rotate_halves(proj(k_off, dk), ck_ref[...], sk_ref[...])
        v = proj(v_off, dv).astype(BF16)
        scores = lax.dot_general(qr.astype(BF16), kr.astype(BF16), NT_DIMS, preferred_element_type=F32)
        scores = (scores * dmask_ref[hd]).astype(BF16)
        state = state_ref[hd]
        out = (jnp.dot(scores, v, preferred_element_type=F32)
               + jnp.dot((qr * xi_ref[hd]).astype(BF16), state.astype(BF16), preferred_element_type=F32))
        kz = (kr * zeta_ref[hd]).astype(BF16)
        state_ref[hd] = state * (_ret_gamma(hd) ** chunk) + lax.dot_general(
            kz, v, TN_DIMS, preferred_element_type=F32)
        out = out * _rms_scale(out)
        gate = proj(g_off, dv)
        gated_ref[:, hd * dv:(hd + 1) * dv] = (out * (gate * jax.nn.sigmoid(gate))).astype(BF16)
    y = jnp.dot(gated_ref[...], wo_ref[...], preferred_element_type=F32)
    o_ref[0] = x_ref[0] + y


def _ret_tables(s, chunk, heads, dk):
    half = dk // 2
    freq = 1.0 / (RET_ROT_BASE ** jnp.linspace(0.0, 1.0, half, dtype=F32))
    ang = jnp.arange(s, dtype=F32)[:, None] * freq[None, :]
    cos, sin = jnp.cos(ang), jnp.sin(ang)
    k_scale = dk ** -0.5
    log_gamma = jnp.log(1.0 - 2.0 ** (-5.0 - jnp.arange(heads, dtype=F32)))
    n = jnp.arange(chunk, dtype=F32)
    diff = n[:, None] - n[None, :]
    dmask = jnp.where(diff[None] >= 0, jnp.exp(jnp.maximum(diff, 0.0)[None] * log_gamma[:, None, None]), 0.0)
    xi = jnp.exp((n[None, :] + 1.0) * log_gamma[:, None])
    zeta = jnp.exp((chunk - 1.0 - n[None, :]) * log_gamma[:, None])
    widen = lambda t: jnp.broadcast_to(t[:, :, None], (heads, chunk, dk))
    return cos, sin, cos * k_scale, sin * k_scale, widen(xi), widen(zeta), dmask


def _deinterleave_heads(w, heads, dk):
    d = w.shape[0]
    return w.reshape(d, heads, dk // 2, 2).transpose(0, 1, 3, 2).reshape(d, heads * dk)


def _retention_layer(x, g, w_in, w_out, *, chunk=MXU_TILE):
    b, s, d = x.shape
    heads = d // RET_DK
    dk, dv = RET_DK, RET_DV_FACTOR * d // heads
    assert s % chunk == 0
    dq = heads * dk
    win = jnp.concatenate([_deinterleave_heads(w_in[:, :dq], heads, dk),
                           _deinterleave_heads(w_in[:, dq:2 * dq], heads, dk),
                           w_in[:, 2 * dq:]], axis=1).astype(BF16)
    cq, sq, ck, sk, xi, zeta, dmask = _ret_tables(s, chunk, heads, dk)
    rot_spec = pl.BlockSpec((chunk, dk // 2), lambda i, j: (j, 0))
    return pl.pallas_call(
        functools.partial(_ret_body, heads=heads, dk=dk, dv=dv),
        grid=(b, s // chunk),
        in_specs=[
            pl.BlockSpec((1, chunk, d), lambda i, j: (i, j, 0)),
            _resident((1, d)),
            rot_spec, rot_spec, rot_spec, rot_spec,
            _resident((heads, chunk, dk)),
            _resident((heads, chunk, dk)),
            _resident((heads, chunk, chunk)),
            _resident(win.shape),
            _resident(w_out.shape),
        ],
        out_specs=pl.BlockSpec((1, chunk, d), lambda i, j: (i, j, 0)),
        out_shape=jax.ShapeDtypeStruct((b, s, d), F32),
        scratch_shapes=[pltpu.VMEM((chunk, d), BF16), pltpu.VMEM((chunk, heads * dv), BF16),
                        pltpu.VMEM((heads, dk, dv), F32)],
        compiler_params=pltpu.CompilerParams(
            dimension_semantics=("parallel", "arbitrary"), vmem_limit_bytes=VMEM_LIMIT_BYTES),
        name="retention",
    )(x, g.reshape(1, d), cq, sq, ck, sk, xi, zeta, dmask, win, w_out.astype(BF16))


ATT_HD = 64
ATT_GROUP = 8
SLAB_HEADS = MXU_TILE // ATT_HD
WINDOW = 128
ROPE_THETA = 10000.0
NEG_INF = -1e30


def _rope_tables(s, scale):
    half = ATT_HD // 2
    inv_freq = 1.0 / (ROPE_THETA ** (jnp.arange(0, ATT_HD, 2, dtype=F32) / ATT_HD))
    ang = jnp.arange(s, dtype=F32)[:, None] * inv_freq[None, :]
    return jnp.tile(jnp.cos(ang), (1, SLAB_HEADS)) * scale, jnp.tile(jnp.sin(ang), (1, SLAB_HEADS)) * scale


def _slab_gain(g):
    half = ATT_HD // 2
    return jnp.concatenate([jnp.tile(g[:half], SLAB_HEADS), jnp.tile(g[half:], SLAB_HEADS)]).reshape(1, MXU_TILE)


def _kv_body(x_ref, g_ref, kg_ref, cos_ref, sin_ref, wk_ref, wv_ref, k_ref, v_ref, *, kv_heads):
    x = x_ref[0]
    h = (x * _rms_scale(x) * g_ref[...]).astype(BF16)
    for g in range(kv_heads):
        sl = slice(g * MXU_TILE, (g + 1) * MXU_TILE)
        k = jnp.dot(h, wk_ref[:, sl], preferred_element_type=F32)
        k = k * _rms_scale(k) * kg_ref[...]
        k_ref[0, :, sl] = _rotate_halves(k, cos_ref[...], sin_ref[...]).astype(BF16)
        v_ref[0, :, sl] = jnp.dot(h, wv_ref[:, sl], preferred_element_type=F32).astype(BF16)


def _shared_kv(x, kv_norm, kv_w, k_norm, *, ts=512):
    b, s, d = x.shape
    kv_heads = kv_w.shape[1] // (2 * ATT_HD)
    half = ATT_HD // 2
    wk = kv_w[:, :kv_heads * ATT_HD].reshape(d, kv_heads, ATT_HD)
    wv = kv_w[:, kv_heads * ATT_HD:].reshape(d, kv_heads, ATT_HD)
    wk = jnp.concatenate([jnp.tile(wk[:, :, :half], (1, 1, SLAB_HEADS)),
                          jnp.tile(wk[:, :, half:], (1, 1, SLAB_HEADS))], axis=2)
    wk = wk.reshape(d, kv_heads * MXU_TILE).astype(BF16)
    wv = jnp.tile(wv, (1, 1, SLAB_HEADS)).reshape(d, kv_heads * MXU_TILE).astype(BF16)
    cos, sin = _rope_tables(s, 1.0)
    width = kv_heads * MXU_TILE
    rot_spec = pl.BlockSpec((ts, MXU_TILE // 2), lambda i, j: (j, 0))
    out_spec = pl.BlockSpec((1, ts, width), lambda i, j: (i, j, 0))
    return pl.pallas_call(
        functools.partial(_kv_body, kv_heads=kv_heads),
        grid=(b, s // ts),
        in_specs=[pl.BlockSpec((1, ts, d), lambda i, j: (i, j, 0)), _resident((1, d)), _resident((1, MXU_TILE)),
                  rot_spec, rot_spec, _resident(wk.shape), _resident(wv.shape)],
        out_specs=[out_spec, out_spec],
        out_shape=[jax.ShapeDtypeStruct((b, s, width), BF16)] * 2,
        compiler_params=pltpu.CompilerParams(
            dimension_semantics=("parallel", "parallel"), vmem_limit_bytes=VMEM_LIMIT_BYTES),
        name="shared_kv",
    )(x, kv_norm.reshape(1, d), _slab_gain(k_norm), cos, sin, wk, wv)


def _attn_body(x_ref, g_ref, qg_ref, cos_ref, sin_ref, ind_ref, bias_ref, sink_ref, kprev_ref, kcur_ref,
               vprev_ref, vcur_ref, wq_ref, wo_ref, o_ref, h_ref, att_ref):
    tq, d = x_ref.shape[1], x_ref.shape[2]
    n_slabs = d // MXU_TILE
    slabs_per_kv = ATT_GROUP // SLAB_HEADS
    x = x_ref[0]
    h_ref[...] = (x * _rms_scale(x) * g_ref[...]).astype(BF16)
    lane = lax.broadcasted_iota(jnp.int32, (WINDOW, MXU_TILE), 1)
    q_lanes = [(lane % (MXU_TILE // 2)) // (ATT_HD // 2) == j for j in range(SLAB_HEADS)]
    o_lanes = [lane // ATT_HD == j for j in range(SLAB_HEADS)]
    first = jnp.where(pl.program_id(1) == 0, 1, 0)
    for slab in range(n_slabs):
        sl = slice(slab * MXU_TILE, (slab + 1) * MXU_TILE)
        kv_sl = slice((slab // slabs_per_kv) * MXU_TILE, (slab // slabs_per_kv + 1) * MXU_TILE)
        q = jnp.dot(h_ref[...], wq_ref[:, sl], preferred_element_type=F32)
        ss = jnp.dot((q * q).astype(BF16), ind_ref[...], preferred_element_type=F32)
        q = q * lax.rsqrt(ss * (1.0 / ATT_HD) + NORM_EPS) * qg_ref[...]
        q = _rotate_halves(q, cos_ref[...], sin_ref[...])
        for blk in range(tq // WINDOW):
            rows = q[blk * WINDOW:(blk + 1) * WINDOW]
            lhs = jnp.concatenate([jnp.where(m, rows, 0.0) for m in q_lanes], axis=0).astype(BF16)
            if blk == 0:
                keys = jnp.concatenate([kprev_ref[0, :, kv_sl], kcur_ref[0, :WINDOW, kv_sl]], axis=0)
                vals = jnp.concatenate([vprev_ref[0, :, kv_sl], vcur_ref[0, :WINDOW, kv_sl]], axis=0)
                bias = bias_ref[first]
            else:
                keys = kcur_ref[0, (blk - 1) * WINDOW:(blk + 1) * WINDOW, kv_sl]
                vals = vcur_ref[0, (blk - 1) * WINDOW:(blk + 1) * WINDOW, kv_sl]
                bias = bias_ref[0]
            sc = lax.dot_general(lhs, keys, NT_DIMS, preferred_element_type=F32) + bias
            sink = sink_ref[slab]
            m = jnp.maximum(jnp.max(sc, axis=-1, keepdims=True), sink)
            p = jnp.exp(sc - m)
            denom = jnp.sum(p, axis=-1, keepdims=True) + jnp.exp(sink - m)
            pv = jnp.dot(p.astype(BF16), vals, preferred_element_type=F32) * (1.0 / denom)
            o = pv[(SLAB_HEADS - 1) * WINDOW:]
            for j in range(SLAB_HEADS - 2, -1, -1):
                o = jnp.where(o_lanes[j], pv[j * WINDOW:(j + 1) * WINDOW], o)
            att_ref[blk * WINDOW:(blk + 1) * WINDOW, sl] = o.astype(BF16)
    y = jnp.dot(att_ref[...], wo_ref[...], preferred_element_type=F32)
    o_ref[0] = x_ref[0] + y


def _attention_layer(x, g, k4, v4, w_q, q_norm, sinks, w_o, *, tq=256):
    b, s, d = x.shape
    heads = d // ATT_HD
    n_slabs = heads // SLAB_HEADS
    half = ATT_HD // 2
    nblk = tq // WINDOW
    assert s % tq == 0
    wq = w_q.reshape(d, n_slabs, SLAB_HEADS, 2, half).transpose(0, 1, 3, 2, 4).reshape(d, d).astype(BF16)
    cos, sin = _rope_tables(s, ATT_HD ** -0.5)
    lane = jnp.arange(MXU_TILE)
    head_of = (lane % (MXU_TILE // 2)) // half
    ind = (head_of[:, None] == head_of[None, :]).astype(BF16)
    qi = jnp.tile(jnp.arange(WINDOW), SLAB_HEADS)[:, None]
    kj = jnp.arange(2 * WINDOW)[None, :]
    band = (kj > qi) & (kj <= qi + WINDOW)
    bias = jnp.stack([jnp.where(band, 0.0, NEG_INF), jnp.where(band & (kj >= WINDOW), 0.0, NEG_INF)]).astype(F32)
    sink_rows = jnp.repeat(sinks.astype(F32).reshape(n_slabs, SLAB_HEADS), WINDOW, axis=1)[:, :, None]
    kv_width = k4.shape[2]
    rot_spec = pl.BlockSpec((tq, MXU_TILE // 2), lambda i, j: (j, 0))
    cur_spec = pl.BlockSpec((1, tq, kv_width), lambda i, j: (i, j, 0))
    prev_spec = pl.BlockSpec((1, WINDOW, kv_width), lambda i, j: (i, jnp.maximum(j * nblk - 1, 0), 0))
    return pl.pallas_call(
        _attn_body,
        grid=(b, s // tq),
        in_specs=[
            pl.BlockSpec((1, tq, d), lambda i, j: (i, j, 0)),
            _resident((1, d)), _resident((1, MXU_TILE)),
            rot_spec, rot_spec,
            _resident(ind.shape), _resident(bias.shape), _resident(sink_rows.shape),
            prev_spec, cur_spec, prev_spec, cur_spec,
            _resident(wq.shape), _resident(w_o.shape),
        ],
        out_specs=pl.BlockSpec((1, tq, d), lambda i, j: (i, j, 0)),
        out_shape=jax.ShapeDtypeStruct((b, s, d), F32),
        scratch_shapes=[pltpu.VMEM((tq, d), BF16), pltpu.VMEM((tq, d), BF16)],
        compiler_params=pltpu.CompilerParams(
            dimension_semantics=("parallel", "parallel"), vmem_limit_bytes=VMEM_LIMIT_BYTES),
        name="swa_attention",
    )(x, g.reshape(1, d), _slab_gain(q_norm), cos, sin, ind, bias, sink_rows, k4, k4, v4, v4, wq,
      w_o.astype(BF16))


def _ffn_layer(x, g, w_in, w_out):
    b, s, d = x.shape
    d_ff = w_out.shape[0]
    y = _ffn(x.reshape(b * s, d), g, w_in[:, :d_ff].astype(BF16), w_in[:, d_ff:].astype(BF16), w_out.astype(BF16))
    return y.reshape(b, s, d)


def kernel(x, ffn1_norm, ffn1_w_in, ffn1_w_out, mix_norm, ffn2_norm, ffn2_w_in, ffn2_w_out,
           ret_w_in, ret_w_out, kv_norm, kv_w, k_norm, attn_w_q, q_norm, attn_sinks, attn_w_o):
    depth = ffn1_norm.shape[0]
    n_ret = ret_w_in.shape[0]
    k4 = v4 = None
    for i in range(depth):
        x = _ffn_layer(x, ffn1_norm[i], ffn1_w_in[i], ffn1_w_out[i])
        if i < n_ret:
            x = _retention_layer(x, mix_norm[i], ret_w_in[i], ret_w_out[i])
        else:
            j = i - n_ret
            x = _attention_layer(x, mix_norm[i], k4, v4, attn_w_q[j], q_norm[j], attn_sinks[j], attn_w_o[j])
        x = _ffn_layer(x, ffn2_norm[i], ffn2_w_in[i], ffn2_w_out[i])
        if i == n_ret - 1:
            k4, v4 = _shared_kv(x, kv_norm, kv_w, k_norm)
    return x
```

```python
import functools

import jax
import jax.numpy as jnp
import numpy as np
from jax import lax
from jax.experimental import pallas as pl
from jax.experimental.pallas import tpu as pltpu

F32 = jnp.float32
BF16 = jnp.bfloat16

NORM_EPS = 1e-6
MXU_TILE = 256
VMEM_LIMIT_BYTES = 56 * 1024 * 1024


def _resident(shape):
    return pl.BlockSpec(shape, lambda *_: (0,) * len(shape), pipeline_mode=pl.Buffered(1))


def _rms_scale(x):
    return lax.rsqrt(jnp.mean(x * x, axis=-1, keepdims=True) + NORM_EPS)


def _ffn_body(x_ref, g_ref, wg_ref, wu_ref, wo_ref, o_ref, h_ref, a_ref, *, ff_chunk):
    x = x_ref[...]
    h_ref[...] = (x * _rms_scale(x) * g_ref[...]).astype(BF16)
    d_ff = wg_ref.shape[1]
    for c in range(d_ff // ff_chunk):
        sl = slice(c * ff_chunk, (c + 1) * ff_chunk)
        gate = jnp.dot(h_ref[...], wg_ref[:, sl], preferred_element_type=F32)
        up = jnp.dot(h_ref[...], wu_ref[:, sl], preferred_element_type=F32)
        a_ref[:, sl] = (gate * jax.nn.sigmoid(gate) * up).astype(BF16)
    y = jnp.dot(a_ref[...], wo_ref[...], preferred_element_type=F32)
    o_ref[...] = x_ref[...] + 0.5 * y


def _ffn(x2d, g, wg, wu, wo, *, tm=512, ff_chunk=MXU_TILE):
    t, d = x2d.shape
    d_ff = wg.shape[1]
    assert t % tm == 0 and d_ff % ff_chunk == 0
    return pl.pallas_call(
        functools.partial(_ffn_body, ff_chunk=ff_chunk),
        grid=(t // tm,),
        in_specs=[
            pl.BlockSpec((tm, d), lambda i: (i, 0)),
            _resident((1, d)),
            _resident((d, d_ff)),
            _resident((d, d_ff)),
            _resident((d_ff, d)),
        ],
        out_specs=pl.BlockSpec((tm, d), lambda i: (i, 0)),
        out_shape=jax.ShapeDtypeStruct((t, d), F32),
        scratch_shapes=[pltpu.VMEM((tm, d), BF16), pltpu.VMEM((tm, d_ff), BF16)],
        compiler_params=pltpu.CompilerParams(
            dimension_semantics=("parallel",), vmem_limit_bytes=VMEM_LIMIT_BYTES),
        name="ffn",
    )(x2d, g.reshape(1, d), wg, wu, wo)


RET_DK = 256
RET_DV_FACTOR = 2
RET_ROT_BASE = 10000.0
NT_DIMS = (((1,), (1,)), ((), ()))
TN_DIMS = (((0,), (0,)), ((), ()))


def _ret_gamma(h):
    return 1.0 - 2.0 ** (-5.0 - h)


def _rotate_halves(t, cos, sin):
    half = t.shape[1] // 2
    te, to = t[:, :half], t[:, half:]
    return jnp.concatenate([te * cos - to * sin, to * cos + te * sin], axis=1)


def _ret_body(x_ref, g_ref, cq_ref, sq_ref, ck_ref, sk_ref, xi_ref, zeta_ref, dmask_ref, win_ref, wo_ref,
              o_ref, h_ref, gated_ref, state_ref, *, heads, dk, dv):
    chunk = x_ref.shape[1]

    @pl.when(pl.program_id(1) == 0)
    def _():
        state_ref[...] = jnp.zeros_like(state_ref)

    x = x_ref[0]
    h_ref[...] = (x * _rms_scale(x) * g_ref[...]).astype(BF16)
    k_off, v_off, g_off = heads * dk, 2 * heads * dk, 2 * heads * dk + heads * dv
    for hd in range(heads):
        proj = lambda off, width: jnp.dot(
            h_ref[...], win_ref[:, off + hd * width: off + (hd + 1) * width], preferred_element_type=F32)
        qr = _rotate_halves(proj(0, dk), cq_ref[...], sq_ref[...])
        kr = _rotate_halves(proj(k_off, dk), ck_ref[...], sk_ref[...])
        v = proj(v_off, dv).astype(BF16)
        scores = lax.dot_general(qr.astype(BF16), kr.astype(BF16), NT_DIMS, preferred_element_type=F32)
        scores = (scores * dmask_ref[hd]).astype(BF16)
        state = state_ref[hd]
        out = (jnp.dot(scores, v, preferred_element_type=F32)
               + jnp.dot((qr * xi_ref[hd]).astype(BF16), state.astype(BF16), preferred_element_type=F32))
        kz = (kr * zeta_ref[hd]).astype(BF16)
        state_ref[hd] = state * (_ret_gamma(hd) ** chunk) + lax.dot_general(
            kz, v, TN_DIMS, preferred_element_type=F32)
        out = out * _rms_scale(out)
        gate = proj(g_off, dv)
        gated_ref[:, hd * dv:(hd + 1) * dv] = (out * (gate * jax.nn.sigmoid(gate))).astype(BF16)
    y = jnp.dot(gated_ref[...], wo_ref[...], preferred_element_type=F32)
    o_ref[0] = x_ref[0] + y


def _ret_tables(s, chunk, heads, dk):
    half = dk // 2
    freq = 1.0 / (RET_ROT_BASE ** np.linspace(0.0, 1.0, half))
    ang = np.arange(s)[:, None] * freq[None, :]
    cos, sin = np.cos(ang), np.sin(ang)
    k_scale = dk ** -0.5
    gamma = np.array([_ret_gamma(h) for h in range(heads)])
    n = np.arange(chunk)
    diff = n[:, None] - n[None, :]
    dmask = np.where(diff[None] >= 0, gamma[:, None, None] ** np.maximum(diff, 0)[None], 0.0)
    xi = gamma[:, None] ** (n[None, :] + 1.0)
    zeta = gamma[:, None] ** (chunk - 1.0 - n[None, :])
    widen = lambda t: np.broadcast_to(t[:, :, None], (heads, chunk, dk))
    tables = (cos, sin, cos * k_scale, sin * k_scale, widen(xi), widen(zeta), dmask)
    return tuple(jnp.asarray(t, F32) for t in tables)


def _deinterleave_heads(w, heads, dk):
    d = w.shape[0]
    return w.reshape(d, heads, dk // 2, 2).transpose(0, 1, 3, 2).reshape(d, heads * dk)


def _retention_layer(x, g, w_in, w_out, *, chunk=MXU_TILE):
    b, s, d = x.shape
    heads = d // RET_DK
    dk, dv = RET_DK, RET_DV_FACTOR * d // heads
    assert s % chunk == 0
    dq = heads * dk
    win = jnp.concatenate([_deinterleave_heads(w_in[:, :dq], heads, dk),
                           _deinterleave_heads(w_in[:, dq:2 * dq], heads, dk),
                           w_in[:, 2 * dq:]], axis=1).astype(BF16)
    cq, sq, ck, sk, xi, zeta, dmask = _ret_tables(s, chunk, heads, dk)
    rot_spec = pl.BlockSpec((chunk, dk // 2), lambda i, j: (j, 0))
    return pl.pallas_call(
        functools.partial(_ret_body, heads=heads, dk=dk, dv=dv),
        grid=(b, s // chunk),
        in_specs=[
            pl.BlockSpec((1, chunk, d), lambda i, j: (i, j, 0)),
            _resident((1, d)),
            rot_spec, rot_spec, rot_spec, rot_spec,
            _resident((heads, chunk, dk)),
            _resident((heads, chunk, dk)),
            _resident((heads, chunk, chunk)),
            _resident(win.shape),
            _resident(w_out.shape),
        ],
        out_specs=pl.BlockSpec((1, chunk, d), lambda i, j: (i, j, 0)),
        out_shape=jax.ShapeDtypeStruct((b, s, d), F32),
        scratch_shapes=[pltpu.VMEM((chunk, d), BF16), pltpu.VMEM((chunk, heads * dv), BF16),
                        pltpu.VMEM((heads, dk, dv), F32)],
        compiler_params=pltpu.CompilerParams(
            dimension_semantics=("parallel", "arbitrary"), vmem_limit_bytes=VMEM_LIMIT_BYTES),
        name="retention",
    )(x, g.reshape(1, d), cq, sq, ck, sk, xi, zeta, dmask, win, w_out.astype(BF16))


ATT_HD = 64
ATT_GROUP = 8
SLAB_HEADS = MXU_TILE // ATT_HD
WINDOW = 128
ROPE_THETA = 10000.0
NEG_INF = -1e30
LOG2_E = 1.4426950408889634


def _rope_tables(s, scale):
    half = ATT_HD // 2
    inv_freq = 1.0 / (ROPE_THETA ** (np.arange(0, ATT_HD, 2) / ATT_HD))
    ang = np.arange(s)[:, None] * inv_freq[None, :]
    return (jnp.asarray(np.tile(np.cos(ang), (1, SLAB_HEADS)) * scale, F32),
            jnp.asarray(np.tile(np.sin(ang), (1, SLAB_HEADS)) * scale, F32))


def _slab_gain(g):
    half = ATT_HD // 2
    return jnp.concatenate([jnp.tile(g[:half], SLAB_HEADS), jnp.tile(g[half:], SLAB_HEADS)]).reshape(1, MXU_TILE)


def _kv_body(x_ref, g_ref, kg_ref, cos_ref, sin_ref, wk_ref, wv_ref, k_ref, vt_ref, *, kv_heads):
    x = x_ref[0]
    h = (x * _rms_scale(x) * g_ref[...]).astype(BF16)
    for g in range(kv_heads):
        sl = slice(g * MXU_TILE, (g + 1) * MXU_TILE)
        k = jnp.dot(h, wk_ref[:, sl], preferred_element_type=F32)
        k = k * _rms_scale(k) * kg_ref[...]
        k_ref[0, :, sl] = _rotate_halves(k, cos_ref[...], sin_ref[...]).astype(BF16)
    v = jnp.dot(h, wv_ref[...], preferred_element_type=F32)
    vt_ref[0] = v.T.astype(BF16)


def _shared_kv(x, kv_norm, kv_w, k_norm, *, ts=512):
    b, s, d = x.shape
    kv_heads = kv_w.shape[1] // (2 * ATT_HD)
    half = ATT_HD // 2
    wk = kv_w[:, :kv_heads * ATT_HD].reshape(d, kv_heads, ATT_HD)
    wk = jnp.concatenate([jnp.tile(wk[:, :, :half], (1, 1, SLAB_HEADS)),
                          jnp.tile(wk[:, :, half:], (1, 1, SLAB_HEADS))], axis=2)
    wk = wk.reshape(d, kv_heads * MXU_TILE).astype(BF16)
    wv = kv_w[:, kv_heads * ATT_HD:].astype(BF16)
    cos, sin = _rope_tables(s, 1.0)
    rot_spec = pl.BlockSpec((ts, MXU_TILE // 2), lambda i, j: (j, 0))
    return pl.pallas_call(
        functools.partial(_kv_body, kv_heads=kv_heads),
        grid=(b, s // ts),
        in_specs=[pl.BlockSpec((1, ts, d), lambda i, j: (i, j, 0)), _resident((1, d)), _resident((1, MXU_TILE)),
                  rot_spec, rot_spec, _resident(wk.shape), _resident(wv.shape)],
        out_specs=[pl.BlockSpec((1, ts, kv_heads * MXU_TILE), lambda i, j: (i, j, 0)),
                   pl.BlockSpec((1, kv_heads * ATT_HD, ts), lambda i, j: (i, 0, j))],
        out_shape=[jax.ShapeDtypeStruct((b, s, kv_heads * MXU_TILE), BF16),
                   jax.ShapeDtypeStruct((b, kv_heads * ATT_HD, s), BF16)],
        compiler_params=pltpu.CompilerParams(
            dimension_semantics=("parallel", "parallel"), vmem_limit_bytes=VMEM_LIMIT_BYTES),
        name="shared_kv",
    )(x, kv_norm.reshape(1, d), _slab_gain(k_norm), cos, sin, wk, wv)


def _attn_body(x_ref, g_ref, qg_ref, cos_ref, sin_ref, ind_ref, bias_ref, sink_ref, kprev_ref, kcur_ref,
               vprev_ref, vcur_ref, wq_ref, wo_ref, o_ref, h_ref, lhs_ref, att_ref):
    tq, d = x_ref.shape[1], x_ref.shape[2]
    n_slabs = d // MXU_TILE
    slabs_per_kv = ATT_GROUP // SLAB_HEADS
    n_kv = n_slabs // slabs_per_kv
    n_blk = tq // WINDOW
    x = x_ref[0]
    h_ref[...] = (x * _rms_scale(x) * g_ref[...]).astype(BF16)
    lane = lax.broadcasted_iota(jnp.int32, (WINDOW, MXU_TILE), 1)
    q_lanes = [(lane % (MXU_TILE // 2)) // (ATT_HD // 2) == j for j in range(SLAB_HEADS)]
    first = jnp.where(pl.program_id(1) == 0, 1, 0)

    q_all = jnp.dot(h_ref[...], wq_ref[...], preferred_element_type=F32)
    sq_all = (q_all * q_all).astype(BF16)
    ss_all = [jnp.dot(sq_all[:, slab * MXU_TILE:(slab + 1) * MXU_TILE], ind_ref[...], preferred_element_type=F32)
              for slab in range(n_slabs)]
    for slab in range(n_slabs):
        q = q_all[:, slab * MXU_TILE:(slab + 1) * MXU_TILE]
        q = q * lax.rsqrt(ss_all[slab] * (1.0 / ATT_HD) + NORM_EPS) * qg_ref[...]
        q = _rotate_halves(q, cos_ref[...], sin_ref[...])
        kv, sub = divmod(slab, slabs_per_kv)
        for blk in range(n_blk):
            rows = q[blk * WINDOW:(blk + 1) * WINDOW]
            for j in range(SLAB_HEADS):
                r0 = (sub * SLAB_HEADS + j) * WINDOW
                lhs_ref[blk, kv, r0:r0 + WINDOW, :] = jnp.where(q_lanes[j], rows, 0.0).astype(BF16)

    units = [(blk, kv) for blk in range(n_blk) for kv in range(n_kv)]

    def scores_t(blk, kv):
        k_sl = slice(kv * MXU_TILE, (kv + 1) * MXU_TILE)
        if blk == 0:
            keys = jnp.concatenate([kprev_ref[0, :, k_sl], kcur_ref[0, :WINDOW, k_sl]], axis=0)
        else:
            keys = kcur_ref[0, (blk - 1) * WINDOW:(blk + 1) * WINDOW, k_sl]
        return lax.dot_general(keys, lhs_ref[blk, kv], NT_DIMS, preferred_element_type=F32)

    st_next = scores_t(*units[0])
    for u, (blk, kv) in enumerate(units):
        st = st_next + (bias_ref[first] if blk == 0 else bias_ref[0])
        if u + 1 < len(units):
            st_next = scores_t(*units[u + 1])
        v_sl = slice(kv * ATT_HD, (kv + 1) * ATT_HD)
        if blk == 0:
            vals_t = jnp.concatenate([vprev_ref[0, v_sl, :], vcur_ref[0, v_sl, :WINDOW]], axis=1)
        else:
            vals_t = vcur_ref[0, v_sl, (blk - 1) * WINDOW:(blk + 1) * WINDOW]
        sink = sink_ref[kv]
        m = jnp.maximum(jnp.max(st, axis=0, keepdims=True), sink)
        p = jnp.exp2(st - m)
        denom = jnp.sum(p, axis=0, keepdims=True) + jnp.exp2(sink - m)
        ot = jnp.dot(vals_t, p.astype(BF16), preferred_element_type=F32) * (1.0 / denom)
        for j in range(ATT_GROUP):
            head = kv * ATT_GROUP + j
            att_ref[head * ATT_HD:(head + 1) * ATT_HD, blk * WINDOW:(blk + 1) * WINDOW] = (
                ot[:, j * WINDOW:(j + 1) * WINDOW].astype(BF16))
    y = lax.dot_general(att_ref[...], wo_ref[...], TN_DIMS, preferred_element_type=F32)
    o_ref[0] = x_ref[0] + y


def _attention_layer(x, g, k4, vt, w_q, q_norm, sinks, w_o, *, tq=512):
    b, s, d = x.shape
    heads = d // ATT_HD
    n_slabs = heads // SLAB_HEADS
    half = ATT_HD // 2
    nblk = tq // WINDOW
    assert s % tq == 0
    wq = w_q.reshape(d, n_slabs, SLAB_HEADS, 2, half).transpose(0, 1, 3, 2, 4).reshape(d, d).astype(BF16)
    cos, sin = _rope_tables(s, ATT_HD ** -0.5 * LOG2_E)
    lane = np.arange(MXU_TILE)
    head_of = (lane % (MXU_TILE // 2)) // half
    ind = jnp.asarray(head_of[:, None] == head_of[None, :], BF16)
    kj = np.arange(2 * WINDOW)[:, None]
    qi = np.tile(np.arange(WINDOW), ATT_GROUP)[None, :]
    band = (kj > qi) & (kj <= qi + WINDOW)
    bias = jnp.asarray(np.stack([np.where(band, 0.0, NEG_INF), np.where(band & (kj >= WINDOW), 0.0, NEG_INF)]), F32)
    n_kv = heads // ATT_GROUP
    sink_rows = jnp.repeat(sinks.astype(F32).reshape(n_kv, ATT_GROUP) * LOG2_E, WINDOW, axis=1)[:, None, :]
    rot_spec = pl.BlockSpec((tq, MXU_TILE // 2), lambda i, j: (j, 0))
    prev_blk = lambda j: jnp.maximum(j * nblk - 1, 0)
    return pl.pallas_call(
        _attn_body,
        grid=(b, s // tq),
        in_specs=[
            pl.BlockSpec((1, tq, d), lambda i, j: (i, j, 0)),
            _resident((1, d)), _resident((1, MXU_TILE)),
            rot_spec, rot_spec,
            _resident(ind.shape), _resident(bias.shape), _resident(sink_rows.shape),
            pl.BlockSpec((1, WINDOW, k4.shape[2]), lambda i, j: (i, prev_blk(j), 0)),
            pl.BlockSpec((1, tq, k4.shape[2]), lambda i, j: (i, j, 0)),
            pl.BlockSpec((1, vt.shape[1], WINDOW), lambda i, j: (i, 0, prev_blk(j))),
            pl.BlockSpec((1, vt.shape[1], tq), lambda i, j: (i, 0, j)),
            _resident(wq.shape), _resident(w_o.shape),
        ],
        out_specs=pl.BlockSpec((1, tq, d), lambda i, j: (i, j, 0)),
        out_shape=jax.ShapeDtypeStruct((b, s, d), F32),
        scratch_shapes=[pltpu.VMEM((tq, d), BF16),
                        pltpu.VMEM((nblk, n_kv, ATT_GROUP * WINDOW, MXU_TILE), BF16),
                        pltpu.VMEM((d, tq), BF16)],
        compiler_params=pltpu.CompilerParams(
            dimension_semantics=("parallel", "parallel"), vmem_limit_bytes=VMEM_LIMIT_BYTES),
        name="swa_attention",
    )(x, g.reshape(1, d), _slab_gain(q_norm), cos, sin, ind, bias, sink_rows, k4, k4, vt, vt, wq,
      w_o.astype(BF16))


def _ffn_layer(x, g, w_in, w_out):
    b, s, d = x.shape
    d_ff = w_out.shape[0]
    y = _ffn(x.reshape(b * s, d), g, w_in[:, :d_ff].astype(BF16), w_in[:, d_ff:].astype(BF16), w_out.astype(BF16))
    return y.reshape(b, s, d)


def kernel(x, ffn1_norm, ffn1_w_in, ffn1_w_out, mix_norm, ffn2_norm, ffn2_w_in, ffn2_w_out,
           ret_w_in, ret_w_out, kv_norm, kv_w, k_norm, attn_w_q, q_norm, attn_sinks, attn_w_o):
    depth = ffn1_norm.shape[0]
    n_ret = ret_w_in.shape[0]
    k4 = v4 = None
    for i in range(depth):
        x = _ffn_layer(x, ffn1_norm[i], ffn1_w_in[i], ffn1_w_out[i])
        if i < n_ret:
            x = _retention_layer(x, mix_norm[i], ret_w_in[i], ret_w_out[i])
        else:
            j = i - n_ret
            x = _attention_layer(x, mix_norm[i], k4, v4, attn_w_q[j], q_norm[j], attn_sinks[j], attn_w_o[j])
        x = _ffn_layer(x, ffn2_norm[i], ffn2_w_in[i], ffn2_w_out[i])
        if i == n_ret - 1:
            k4, v4 = _shared_kv(x, kv_norm, kv_w, k_norm)
    return x
```

```python
import functools

import jax
import jax.numpy as jnp
import numpy as np
from jax import lax
from jax.experimental import pallas as pl
from jax.experimental.pallas import tpu as pltpu

F32 = jnp.float32
BF16 = jnp.bfloat16

NORM_EPS = 1e-6
MXU_TILE = 256
VMEM_LIMIT_BYTES = 56 * 1024 * 1024


def _resident(shape):
    return pl.BlockSpec(shape, lambda *_: (0,) * len(shape), pipeline_mode=pl.Buffered(1))


def _rms_scale(x):
    return lax.rsqrt(jnp.mean(x * x, axis=-1, keepdims=True) + NORM_EPS)


FFN_NORM_PIECES = 8


def _ffn_body(x_ref, xnext_ref, g_ref, wg_ref, wu_ref, wo_ref, o_ref, h_ref, a_ref, *, ff_chunk):
    tm = x_ref.shape[0]
    n_chunks = wg_ref.shape[1] // ff_chunk
    piece = tm // FFN_NORM_PIECES
    step = pl.program_id(0)
    cur, nxt = step % 2, (step + 1) % 2

    def norm_rows(src_ref, slot, rows):
        x = src_ref[rows, :]
        h_ref[slot, rows, :] = (x * _rms_scale(x) * g_ref[...]).astype(BF16)

    @pl.when(step == 0)
    def _():
        norm_rows(x_ref, 0, slice(0, tm))

    for c in range(n_chunks):
        sl = slice(c * ff_chunk, (c + 1) * ff_chunk)
        gate = jnp.dot(h_ref[cur], wg_ref[:, sl], preferred_element_type=F32)
        up = jnp.dot(h_ref[cur], wu_ref[:, sl], preferred_element_type=F32)
        a_ref[:, sl] = (gate * jax.nn.sigmoid(gate) * up).astype(BF16)
        if c < FFN_NORM_PIECES:
            norm_rows(xnext_ref, nxt, slice(c * piece, (c + 1) * piece))
    y = jnp.dot(a_ref[...], wo_ref[...], preferred_element_type=F32)
    o_ref[...] = x_ref[...] + 0.5 * y


def _ffn(x2d, g, wg, wu, wo, *, tm=1024, ff_chunk=MXU_TILE):
    t, d = x2d.shape
    d_ff = wg.shape[1]
    steps = t // tm
    assert t % tm == 0 and d_ff % ff_chunk == 0 and d_ff // ff_chunk >= FFN_NORM_PIECES
    return pl.pallas_call(
        functools.partial(_ffn_body, ff_chunk=ff_chunk),
        grid=(steps,),
        in_specs=[
            pl.BlockSpec((tm, d), lambda i: (i, 0)),
            pl.BlockSpec((tm, d), lambda i: (jnp.minimum(i + 1, steps - 1), 0)),
            _resident((1, d)),
            _resident((d, d_ff)),
            _resident((d, d_ff)),
            _resident((d_ff, d)),
        ],
        out_specs=pl.BlockSpec((tm, d), lambda i: (i, 0)),
        out_shape=jax.ShapeDtypeStruct((t, d), F32),
        scratch_shapes=[pltpu.VMEM((2, tm, d), BF16), pltpu.VMEM((tm, d_ff), BF16)],
        compiler_params=pltpu.CompilerParams(
            dimension_semantics=("arbitrary",), vmem_limit_bytes=VMEM_LIMIT_BYTES),
        name="ffn",
    )(x2d, x2d, g.reshape(1, d), wg, wu, wo)


RET_DK = 256
RET_DV_FACTOR = 2
RET_ROT_BASE = 10000.0
NT_DIMS = (((1,), (1,)), ((), ()))
TN_DIMS = (((0,), (0,)), ((), ()))


def _ret_gamma(h):
    return 1.0 - 2.0 ** (-5.0 - h)


def _rotate_halves(t, cos, sin):
    half = t.shape[1] // 2
    te, to = t[:, :half], t[:, half:]
    return jnp.concatenate([te * cos - to * sin, to * cos + te * sin], axis=1)


def _ret_body(x_ref, g_ref, cq_ref, sq_ref, ck_ref, sk_ref, xi_ref, zeta_ref, dmask_ref, win_ref, wo_ref,
              o_ref, h_ref, gated_ref, state_ref, *, heads, dk, dv):
    chunk = x_ref.shape[1]

    @pl.when(pl.program_id(1) == 0)
    def _():
        state_ref[...] = jnp.zeros_like(state_ref)

    x = x_ref[0]
    h_ref[...] = (x * _rms_scale(x) * g_ref[...]).astype(BF16)
    k_off, v_off, g_off = heads * dk, 2 * heads * dk, 2 * heads * dk + heads * dv
    for hd in range(heads):
        proj = lambda off, width: jnp.dot(
            h_ref[...], win_ref[:, off + hd * width: off + (hd + 1) * width], preferred_element_type=F32)
        qr = _rotate_halves(proj(0, dk), cq_ref[...], sq_ref[...])
        kr = _rotate_halves(proj(k_off, dk), ck_ref[...], sk_ref[...])
        v = proj(v_off, dv).astype(BF16)
        scores = lax.dot_general(qr.astype(BF16), kr.astype(BF16), NT_DIMS, preferred_element_type=F32)
        scores = (scores * dmask_ref[hd]).astype(BF16)
        state = state_ref[hd]
        out = (jnp.dot(scores, v, preferred_element_type=F32)
               + jnp.dot((qr * xi_ref[hd]).astype(BF16), state.astype(BF16), preferred_element_type=F32))
        kz = (kr * zeta_ref[hd]).astype(BF16)
        state_ref[hd] = state * (_ret_gamma(hd) ** chunk) + lax.dot_general(
            kz, v, TN_DIMS, preferred_element_type=F32)
        out = out * _rms_scale(out)
        gate = proj(g_off, dv)
        gated_ref[:, hd * dv:(hd + 1) * dv] = (out * (gate * jax.nn.sigmoid(gate))).astype(BF16)
    y = jnp.dot(gated_ref[...], wo_ref[...], preferred_element_type=F32)
    o_ref[0] = x_ref[0] + y


def _ret_tables(s, chunk, heads, dk):
    half = dk // 2
    freq = 1.0 / (RET_ROT_BASE ** np.linspace(0.0, 1.0, half))
    ang = np.arange(s)[:, None] * freq[None, :]
    cos, sin = np.cos(ang), np.sin(ang)
    k_scale = dk ** -0.5
    gamma = np.array([_ret_gamma(h) for h in range(heads)])
    n = np.arange(chunk)
    diff = n[:, None] - n[None, :]
    dmask = np.where(diff[None] >= 0, gamma[:, None, None] ** np.maximum(diff, 0)[None], 0.0)
    xi = gamma[:, None] ** (n[None, :] + 1.0)
    zeta = gamma[:, None] ** (chunk - 1.0 - n[None, :])
    widen = lambda t: np.broadcast_to(t[:, :, None], (heads, chunk, dk))
    tables = (cos, sin, cos * k_scale, sin * k_scale, widen(xi), widen(zeta), dmask)
    return tuple(jnp.asarray(t, F32) for t in tables)


def _deinterleave_heads(w, heads, dk):
    d = w.shape[0]
    return w.reshape(d, heads, dk // 2, 2).transpose(0, 1, 3, 2).reshape(d, heads * dk)


def _retention_layer(x, g, w_in, w_out, *, chunk=MXU_TILE):
    b, s, d = x.shape
    heads = d // RET_DK
    dk, dv = RET_DK, RET_DV_FACTOR * d // heads
    assert s % chunk == 0
    dq = heads * dk
    win = jnp.concatenate([_deinterleave_heads(w_in[:, :dq], heads, dk),
                           _deinterleave_heads(w_in[:, dq:2 * dq], heads, dk),
                           w_in[:, 2 * dq:]], axis=1).astype(BF16)
    cq, sq, ck, sk, xi, zeta, dmask = _ret_tables(s, chunk, heads, dk)
    rot_spec = pl.BlockSpec((chunk, dk // 2), lambda i, j: (j, 0))
    return pl.pallas_call(
        functools.partial(_ret_body, heads=heads, dk=dk, dv=dv),
        grid=(b, s // chunk),
        in_specs=[
            pl.BlockSpec((1, chunk, d), lambda i, j: (i, j, 0)),
            _resident((1, d)),
            rot_spec, rot_spec, rot_spec, rot_spec,
            _resident((heads, chunk, dk)),
            _resident((heads, chunk, dk)),
            _resident((heads, chunk, chunk)),
            _resident(win.shape),
            _resident(w_out.shape),
        ],
        out_specs=pl.BlockSpec((1, chunk, d), lambda i, j: (i, j, 0)),
        out_shape=jax.ShapeDtypeStruct((b, s, d), F32),
        scratch_shapes=[pltpu.VMEM((chunk, d), BF16), pltpu.VMEM((chunk, heads * dv), BF16),
                        pltpu.VMEM((heads, dk, dv), F32)],
        compiler_params=pltpu.CompilerParams(
            dimension_semantics=("parallel", "arbitrary"), vmem_limit_bytes=VMEM_LIMIT_BYTES),
        name="retention",
    )(x, g.reshape(1, d), cq, sq, ck, sk, xi, zeta, dmask, win, w_out.astype(BF16))


ATT_HD = 64
ATT_GROUP = 8
SLAB_HEADS = MXU_TILE // ATT_HD
WINDOW = 128
ROPE_THETA = 10000.0
NEG_INF = -1e30
LOG2_E = 1.4426950408889634
SCORE_LOOKAHEAD = 2


def _rope_tables(s, scale):
    half = ATT_HD // 2
    inv_freq = 1.0 / (ROPE_THETA ** (np.arange(0, ATT_HD, 2) / ATT_HD))
    ang = np.arange(s)[:, None] * inv_freq[None, :]
    return (jnp.asarray(np.tile(np.cos(ang), (1, SLAB_HEADS)) * scale, F32),
            jnp.asarray(np.tile(np.sin(ang), (1, SLAB_HEADS)) * scale, F32))


def _slab_gain(g):
    half = ATT_HD // 2
    return jnp.concatenate([jnp.tile(g[:half], SLAB_HEADS), jnp.tile(g[half:], SLAB_HEADS)]).reshape(1, MXU_TILE)


def _kv_body(x_ref, g_ref, kg_ref, cos_ref, sin_ref, wk_ref, wv_ref, k_ref, vt_ref, *, kv_heads):
    x = x_ref[0]
    h = (x * _rms_scale(x) * g_ref[...]).astype(BF16)
    for g in range(kv_heads):
        sl = slice(g * MXU_TILE, (g + 1) * MXU_TILE)
        k = jnp.dot(h, wk_ref[:, sl], preferred_element_type=F32)
        k = k * _rms_scale(k) * kg_ref[...]
        k_ref[0, :, sl] = _rotate_halves(k, cos_ref[...], sin_ref[...]).astype(BF16)
    v = jnp.dot(h, wv_ref[...], preferred_element_type=F32)
    vt_ref[0] = v.T.astype(BF16)


def _shared_kv(x, kv_norm, kv_w, k_norm, *, ts=512):
    b, s, d = x.shape
    kv_heads = kv_w.shape[1] // (2 * ATT_HD)
    half = ATT_HD // 2
    wk = kv_w[:, :kv_heads * ATT_HD].reshape(d, kv_heads, ATT_HD)
    wk = jnp.concatenate([jnp.tile(wk[:, :, :half], (1, 1, SLAB_HEADS)),
                          jnp.tile(wk[:, :, half:], (1, 1, SLAB_HEADS))], axis=2)
    wk = wk.reshape(d, kv_heads * MXU_TILE).astype(BF16)
    wv = kv_w[:, kv_heads * ATT_HD:].astype(BF16)
    cos, sin = _rope_tables(s, 1.0)
    rot_spec = pl.BlockSpec((ts, MXU_TILE // 2), lambda i, j: (j, 0))
    return pl.pallas_call(
        functools.partial(_kv_body, kv_heads=kv_heads),
        grid=(b, s // ts),
        in_specs=[pl.BlockSpec((1, ts, d), lambda i, j: (i, j, 0)), _resident((1, d)), _resident((1, MXU_TILE)),
                  rot_spec, rot_spec, _resident(wk.shape), _resident(wv.shape)],
        out_specs=[pl.BlockSpec((1, ts, kv_heads * MXU_TILE), lambda i, j: (i, j, 0)),
                   pl.BlockSpec((1, kv_heads * ATT_HD, ts), lambda i, j: (i, 0, j))],
        out_shape=[jax.ShapeDtypeStruct((b, s, kv_heads * MXU_TILE), BF16),
                   jax.ShapeDtypeStruct((b, kv_heads * ATT_HD, s), BF16)],
        compiler_params=pltpu.CompilerParams(
            dimension_semantics=("parallel", "parallel"), vmem_limit_bytes=VMEM_LIMIT_BYTES),
        name="shared_kv",
    )(x, kv_norm.reshape(1, d), _slab_gain(k_norm), cos, sin, wk, wv)


def _attn_body(x_ref, g_ref, rot_ref, ind_ref, bias_ref, sink_ref, kprev_ref, kcur_ref,
               vprev_ref, vcur_ref, wq_ref, wo_ref, o_ref, h_ref, lhs_ref, att_ref):
    tq, d = x_ref.shape[1], x_ref.shape[2]
    n_slabs = d // MXU_TILE
    slabs_per_kv = ATT_GROUP // SLAB_HEADS
    n_kv = n_slabs // slabs_per_kv
    n_blk = tq // WINDOW
    x = x_ref[0]
    h_ref[...] = (x * _rms_scale(x) * g_ref[...]).astype(BF16)
    lane = lax.broadcasted_iota(jnp.int32, (WINDOW, MXU_TILE), 1)
    q_lanes = [(lane % (MXU_TILE // 2)) // (ATT_HD // 2) == j for j in range(SLAB_HEADS)]
    first = jnp.where(pl.program_id(1) == 0, 1, 0)

    def project(slab):
        return jnp.dot(h_ref[...], wq_ref[:, slab * MXU_TILE:(slab + 1) * MXU_TILE], preferred_element_type=F32)

    half = MXU_TILE // 2
    q_next = project(0)
    for slab in range(n_slabs):
        q = q_next
        if slab + 1 < n_slabs:
            q_next = project(slab + 1)
        ms = jnp.dot((q * q).astype(BF16), ind_ref[...], preferred_element_type=F32)
        q = q * lax.rsqrt(ms + NORM_EPS)
        q1, q2 = q[:, :half], q[:, half:]
        q = jnp.concatenate([q1 * rot_ref[0] - q2 * rot_ref[1], q2 * rot_ref[2] + q1 * rot_ref[3]], axis=1)
        q = q.astype(BF16)
        kv, sub = divmod(slab, slabs_per_kv)
        for blk in range(n_blk):
            rows = q[blk * WINDOW:(blk + 1) * WINDOW]
            for j in range(SLAB_HEADS):
                r0 = (sub * SLAB_HEADS + j) * WINDOW
                lhs_ref[blk, kv, r0:r0 + WINDOW, :] = jnp.where(q_lanes[j], rows, jnp.zeros_like(rows))

    units = [(blk, kv) for blk in range(n_blk) for kv in range(n_kv)]

    def scores_t(blk, kv):
        k_sl = slice(kv * MXU_TILE, (kv + 1) * MXU_TILE)
        if blk == 0:
            keys = jnp.concatenate([kprev_ref[0, :, k_sl], kcur_ref[0, :WINDOW, k_sl]], axis=0)
        else:
            keys = kcur_ref[0, (blk - 1) * WINDOW:(blk + 1) * WINDOW, k_sl]
        return lax.dot_general(keys, lhs_ref[blk, kv], NT_DIMS, preferred_element_type=F32)

    pending = [scores_t(*unit) for unit in units[:SCORE_LOOKAHEAD]]
    for u, (blk, kv) in enumerate(units):
        st = pending.pop(0) + (bias_ref[first] if blk == 0 else bias_ref[0])
        if u + SCORE_LOOKAHEAD < len(units):
            pending.append(scores_t(*units[u + SCORE_LOOKAHEAD]))
        v_sl = slice(kv * ATT_HD, (kv + 1) * ATT_HD)
        if blk == 0:
            vals_t = jnp.concatenate([vprev_ref[0, v_sl, :], vcur_ref[0, v_sl, :WINDOW]], axis=1)
        else:
            vals_t = vcur_ref[0, v_sl, (blk - 1) * WINDOW:(blk + 1) * WINDOW]
        sink = sink_ref[kv]
        m = jnp.maximum(jnp.max(st, axis=0, keepdims=True), sink)
        p = jnp.exp2(st - m)
        denom = jnp.sum(p, axis=0, keepdims=True) + jnp.exp2(sink - m)
        ot = jnp.dot(vals_t, p.astype(BF16), preferred_element_type=F32) * (1.0 / denom)
        for j in range(ATT_GROUP):
            head = kv * ATT_GROUP + j
            att_ref[head * ATT_HD:(head + 1) * ATT_HD, blk * WINDOW:(blk + 1) * WINDOW] = (
                ot[:, j * WINDOW:(j + 1) * WINDOW].astype(BF16))
    y = lax.dot_general(att_ref[...], wo_ref[...], TN_DIMS, preferred_element_type=F32)
    o_ref[0] = x_ref[0] + y


def _attention_layer(x, g, k4, vt, w_q, q_norm, sinks, w_o, *, tq=512):
    b, s, d = x.shape
    heads = d // ATT_HD
    n_slabs = heads // SLAB_HEADS
    half = ATT_HD // 2
    nblk = tq // WINDOW
    assert s % tq == 0
    wq = w_q.reshape(d, n_slabs, SLAB_HEADS, 2, half).transpose(0, 1, 3, 2, 4).reshape(d, d).astype(BF16)
    cos, sin = _rope_tables(s, ATT_HD ** -0.5 * LOG2_E)
    g1, g2 = jnp.tile(q_norm[:half], SLAB_HEADS)[None, :], jnp.tile(q_norm[half:], SLAB_HEADS)[None, :]
    rot = jnp.stack([cos * g1, sin * g2, cos * g2, sin * g1])
    lane = np.arange(MXU_TILE)
    head_of = (lane % (MXU_TILE // 2)) // half
    ind = jnp.asarray((head_of[:, None] == head_of[None, :]) / ATT_HD, BF16)
    kj = np.arange(2 * WINDOW)[:, None]
    qi = np.tile(np.arange(WINDOW), ATT_GROUP)[None, :]
    band = (kj > qi) & (kj <= qi + WINDOW)
    bias = jnp.asarray(np.stack([np.where(band, 0.0, NEG_INF), np.where(band & (kj >= WINDOW), 0.0, NEG_INF)]), F32)
    n_kv = heads // ATT_GROUP
    sink_rows = jnp.repeat(sinks.astype(F32).reshape(n_kv, ATT_GROUP) * LOG2_E, WINDOW, axis=1)[:, None, :]
    prev_blk = lambda j: jnp.maximum(j * nblk - 1, 0)
    return pl.pallas_call(
        _attn_body,
        grid=(b, s // tq),
        in_specs=[
            pl.BlockSpec((1, tq, d), lambda i, j: (i, j, 0)),
            _resident((1, d)),
            pl.BlockSpec((4, tq, MXU_TILE // 2), lambda i, j: (0, j, 0)),
            _resident(ind.shape), _resident(bias.shape), _resident(sink_rows.shape),
            pl.BlockSpec((1, WINDOW, k4.shape[2]), lambda i, j: (i, prev_blk(j), 0)),
            pl.BlockSpec((1, tq, k4.shape[2]), lambda i, j: (i, j, 0)),
            pl.BlockSpec((1, vt.shape[1], WINDOW), lambda i, j: (i, 0, prev_blk(j))),
            pl.BlockSpec((1, vt.shape[1], tq), lambda i, j: (i, 0, j)),
            _resident(wq.shape), _resident(w_o.shape),
        ],
        out_specs=pl.BlockSpec((1, tq, d), lambda i, j: (i, j, 0)),
        out_shape=jax.ShapeDtypeStruct((b, s, d), F32),
        scratch_shapes=[pltpu.VMEM((tq, d), BF16),
                        pltpu.VMEM((nblk, n_kv, ATT_GROUP * WINDOW, MXU_TILE), BF16),
                        pltpu.VMEM((d, tq), BF16)],
        compiler_params=pltpu.CompilerParams(
            dimension_semantics=("parallel", "parallel"), vmem_limit_bytes=VMEM_LIMIT_BYTES),
        name="swa_attention",
    )(x, g.reshape(1, d), rot, ind, bias, sink_rows, k4, k4, vt, vt, wq, w_o.astype(BF16))


def _ffn_layer(x, g, w_in, w_out):
    b, s, d = x.shape
    d_ff = w_out.shape[0]
    y = _ffn(x.reshape(b * s, d), g, w_in[:, :d_ff].astype(BF16), w_in[:, d_ff:].astype(BF16), w_out.astype(BF16))
    return y.reshape(b, s, d)


def kernel(x, ffn1_norm, ffn1_w_in, ffn1_w_out, mix_norm, ffn2_norm, ffn2_w_in, ffn2_w_out,
           ret_w_in, ret_w_out, kv_norm, kv_w, k_norm, attn_w_q, q_norm, attn_sinks, attn_w_o):
    depth = ffn1_norm.shape[0]
    n_ret = ret_w_in.shape[0]
    k4 = v4 = None
    for i in range(depth):
        x = _ffn_layer(x, ffn1_norm[i], ffn1_w_in[i], ffn1_w_out[i])
        if i < n_ret:
            x = _retention_layer(x, mix_norm[i], ret_w_in[i], ret_w_out[i])
        else:
            j = i - n_ret
            x = _attention_layer(x, mix_norm[i], k4, v4, attn_w_q[j], q_norm[j], attn_sinks[j], attn_w_o[j])
        x = _ffn_layer(x, ffn2_norm[i], ffn2_w_in[i], ffn2_w_out[i])
        if i == n_ret - 1:
            k4, v4 = _shared_kv(x, kv_norm, kv_w, k_norm)
    return x
```

```python
import functools

import jax
import jax.numpy as jnp
import numpy as np
from jax import lax
from jax.experimental import pallas as pl
from jax.experimental.pallas import tpu as pltpu

F32 = jnp.float32
BF16 = jnp.bfloat16

NORM_EPS = 1e-6
MXU_TILE = 256
VMEM_LIMIT_BYTES = 56 * 1024 * 1024


def _resident(shape):
    return pl.BlockSpec(shape, lambda *_: (0,) * len(shape), pipeline_mode=pl.Buffered(1))


def _rms_scale(x):
    return lax.rsqrt(jnp.mean(x * x, axis=-1, keepdims=True) + NORM_EPS)


FFN_NORM_PIECES = 8


def _ffn_body(x_ref, xnext_ref, g_ref, wg_ref, wu_ref, wo_ref, o_ref, h_ref, a_ref, *, ff_chunk):
    tm = x_ref.shape[0]
    n_chunks = wg_ref.shape[1] // ff_chunk
    piece = tm // FFN_NORM_PIECES
    step = pl.program_id(0)
    cur, nxt = step % 2, (step + 1) % 2

    def norm_rows(src_ref, slot, rows):
        x = src_ref[rows, :]
        h_ref[slot, rows, :] = (x * _rms_scale(x) * g_ref[...]).astype(BF16)

    @pl.when(step == 0)
    def _():
        norm_rows(x_ref, 0, slice(0, tm))

    for c in range(n_chunks):
        sl = slice(c * ff_chunk, (c + 1) * ff_chunk)
        gate = jnp.dot(h_ref[cur], wg_ref[:, sl], preferred_element_type=F32)
        up = jnp.dot(h_ref[cur], wu_ref[:, sl], preferred_element_type=F32)
        a_ref[:, sl] = (gate * jax.nn.sigmoid(gate) * up).astype(BF16)
        if c < FFN_NORM_PIECES:
            norm_rows(xnext_ref, nxt, slice(c * piece, (c + 1) * piece))
    y = jnp.dot(a_ref[...], wo_ref[...], preferred_element_type=F32)
    o_ref[...] = x_ref[...] + 0.5 * y


def _ffn(x2d, g, wg, wu, wo, *, tm=1024, ff_chunk=MXU_TILE):
    t, d = x2d.shape
    d_ff = wg.shape[1]
    steps = t // tm
    assert t % tm == 0 and d_ff % ff_chunk == 0 and d_ff // ff_chunk >= FFN_NORM_PIECES
    return pl.pallas_call(
        functools.partial(_ffn_body, ff_chunk=ff_chunk),
        grid=(steps,),
        in_specs=[
            pl.BlockSpec((tm, d), lambda i: (i, 0)),
            pl.BlockSpec((tm, d), lambda i: (jnp.minimum(i + 1, steps - 1), 0)),
            _resident((1, d)),
            _resident((d, d_ff)),
            _resident((d, d_ff)),
            _resident((d_ff, d)),
        ],
        out_specs=pl.BlockSpec((tm, d), lambda i: (i, 0)),
        out_shape=jax.ShapeDtypeStruct((t, d), F32),
        scratch_shapes=[pltpu.VMEM((2, tm, d), BF16), pltpu.VMEM((tm, d_ff), BF16)],
        compiler_params=pltpu.CompilerParams(
            dimension_semantics=("arbitrary",), vmem_limit_bytes=VMEM_LIMIT_BYTES),
        name="ffn",
    )(x2d, x2d, g.reshape(1, d), wg, wu, wo)


RET_DK = 256
RET_DV_FACTOR = 2
RET_ROT_BASE = 10000.0
NT_DIMS = (((1,), (1,)), ((), ()))
TN_DIMS = (((0,), (0,)), ((), ()))


def _ret_gamma(h):
    return 1.0 - 2.0 ** (-5.0 - h)


def _rotate_halves(t, cos, sin):
    half = t.shape[1] // 2
    te, to = t[:, :half], t[:, half:]
    return jnp.concatenate([te * cos - to * sin, to * cos + te * sin], axis=1)


def _ret_body(x_ref, g_ref, cq_ref, sq_ref, ck_ref, sk_ref, xi_ref, zeta_ref, dmask_ref, win_ref, wo_ref,
              o_ref, h_ref, gated_ref, state_ref, *, heads, dk, dv, chunk):
    rows = x_ref.shape[1]

    @pl.when(pl.program_id(1) == 0)
    def _():
        state_ref[...] = jnp.zeros_like(state_ref)

    x = x_ref[0]
    h_ref[...] = (x * _rms_scale(x) * g_ref[...]).astype(BF16)
    k_off, v_off, g_off = heads * dk, 2 * heads * dk, 2 * heads * dk + heads * dv
    for hd in range(heads):
        proj = lambda off, width: jnp.dot(
            h_ref[...], win_ref[:, off + hd * width: off + (hd + 1) * width], preferred_element_type=F32)
        qr = _rotate_halves(proj(0, dk), cq_ref[...], sq_ref[...])
        kr = _rotate_halves(proj(k_off, dk), ck_ref[...], sk_ref[...])
        v = proj(v_off, dv).astype(BF16)
        gate = proj(g_off, dv)
        gate = gate * jax.nn.sigmoid(gate)
        state = state_ref[hd]
        for c in range(rows // chunk):
            sl = slice(c * chunk, (c + 1) * chunk)
            qc, kc, vc = qr[sl], kr[sl], v[sl]
            scores = lax.dot_general(qc.astype(BF16), kc.astype(BF16), NT_DIMS, preferred_element_type=F32)
            scores = (scores * dmask_ref[hd]).astype(BF16)
            out = (jnp.dot(scores, vc, preferred_element_type=F32)
                   + jnp.dot((qc * xi_ref[hd]).astype(BF16), state.astype(BF16), preferred_element_type=F32))
            kz = (kc * zeta_ref[hd]).astype(BF16)
            state = state * (_ret_gamma(hd) ** chunk) + lax.dot_general(kz, vc, TN_DIMS, preferred_element_type=F32)
            out = out * _rms_scale(out)
            gated_ref[sl, hd * dv:(hd + 1) * dv] = (out * gate[sl]).astype(BF16)
        state_ref[hd] = state
    y = jnp.dot(gated_ref[...], wo_ref[...], preferred_element_type=F32)
    o_ref[0] = x_ref[0] + y


def _ret_tables(s, chunk, heads, dk):
    half = dk // 2
    freq = 1.0 / (RET_ROT_BASE ** np.linspace(0.0, 1.0, half))
    ang = np.arange(s)[:, None] * freq[None, :]
    cos, sin = np.cos(ang), np.sin(ang)
    k_scale = dk ** -0.5
    gamma = np.array([_ret_gamma(h) for h in range(heads)])
    n = np.arange(chunk)
    diff = n[:, None] - n[None, :]
    dmask = np.where(diff[None] >= 0, gamma[:, None, None] ** np.maximum(diff, 0)[None], 0.0)
    xi = gamma[:, None] ** (n[None, :] + 1.0)
    zeta = gamma[:, None] ** (chunk - 1.0 - n[None, :])
    widen = lambda t: np.broadcast_to(t[:, :, None], (heads, chunk, dk))
    tables = (cos, sin, cos * k_scale, sin * k_scale, widen(xi), widen(zeta), dmask)
    return tuple(jnp.asarray(t, F32) for t in tables)


def _deinterleave_heads(w, heads, dk):
    d = w.shape[0]
    return w.reshape(d, heads, dk // 2, 2).transpose(0, 1, 3, 2).reshape(d, heads * dk)


def _retention_layer(x, g, w_in, w_out, *, chunk=MXU_TILE, ts=2 * MXU_TILE):
    b, s, d = x.shape
    heads = d // RET_DK
    dk, dv = RET_DK, RET_DV_FACTOR * d // heads
    assert s % ts == 0 and ts % chunk == 0
    dq = heads * dk
    win = jnp.concatenate([_deinterleave_heads(w_in[:, :dq], heads, dk),
                           _deinterleave_heads(w_in[:, dq:2 * dq], heads, dk),
                           w_in[:, 2 * dq:]], axis=1).astype(BF16)
    cq, sq, ck, sk, xi, zeta, dmask = _ret_tables(s, chunk, heads, dk)
    rot_spec = pl.BlockSpec((ts, dk // 2), lambda i, j: (j, 0))
    return pl.pallas_call(
        functools.partial(_ret_body, heads=heads, dk=dk, dv=dv, chunk=chunk),
        grid=(b, s // ts),
        in_specs=[
            pl.BlockSpec((1, ts, d), lambda i, j: (i, j, 0)),
            _resident((1, d)),
            rot_spec, rot_spec, rot_spec, rot_spec,
            _resident((heads, chunk, dk)),
            _resident((heads, chunk, dk)),
            _resident((heads, chunk, chunk)),
            _resident(win.shape),
            _resident(w_out.shape),
        ],
        out_specs=pl.BlockSpec((1, ts, d), lambda i, j: (i, j, 0)),
        out_shape=jax.ShapeDtypeStruct((b, s, d), F32),
        scratch_shapes=[pltpu.VMEM((ts, d), BF16), pltpu.VMEM((ts, heads * dv), BF16),
                        pltpu.VMEM((heads, dk, dv), F32)],
        compiler_params=pltpu.CompilerParams(
            dimension_semantics=("parallel", "arbitrary"), vmem_limit_bytes=VMEM_LIMIT_BYTES),
        name="retention",
    )(x, g.reshape(1, d), cq, sq, ck, sk, xi, zeta, dmask, win, w_out.astype(BF16))


ATT_HD = 64
ATT_GROUP = 8
SLAB_HEADS = MXU_TILE // ATT_HD
WINDOW = 128
ROPE_THETA = 10000.0
NEG_INF = -1e30
LOG2_E = 1.4426950408889634
UNIT_HEADS = 8


def _rope_tables(s, scale):
    half = ATT_HD // 2
    inv_freq = 1.0 / (ROPE_THETA ** (np.arange(0, ATT_HD, 2) / ATT_HD))
    ang = np.arange(s)[:, None] * inv_freq[None, :]
    return (jnp.asarray(np.tile(np.cos(ang), (1, SLAB_HEADS)) * scale, F32),
            jnp.asarray(np.tile(np.sin(ang), (1, SLAB_HEADS)) * scale, F32))


def _slab_gain(g):
    half = ATT_HD // 2
    return jnp.concatenate([jnp.tile(g[:half], SLAB_HEADS), jnp.tile(g[half:], SLAB_HEADS)]).reshape(1, MXU_TILE)


def _kv_body(x_ref, g_ref, kg_ref, cos_ref, sin_ref, wk_ref, wv_ref, k_ref, vt_ref, *, kv_heads):
    x = x_ref[0]
    h = (x * _rms_scale(x) * g_ref[...]).astype(BF16)
    for g in range(kv_heads):
        sl = slice(g * MXU_TILE, (g + 1) * MXU_TILE)
        k = jnp.dot(h, wk_ref[:, sl], preferred_element_type=F32)
        k = k * _rms_scale(k) * kg_ref[...]
        k_ref[0, :, sl] = _rotate_halves(k, cos_ref[...], sin_ref[...]).astype(BF16)
    v = jnp.dot(h, wv_ref[...], preferred_element_type=F32)
    vt_ref[0] = v.T.astype(BF16)


def _shared_kv(x, kv_norm, kv_w, k_norm, *, ts=1024):
    b, s, d = x.shape
    kv_heads = kv_w.shape[1] // (2 * ATT_HD)
    half = ATT_HD // 2
    wk = kv_w[:, :kv_heads * ATT_HD].reshape(d, kv_heads, ATT_HD)
    wk = jnp.concatenate([jnp.tile(wk[:, :, :half], (1, 1, SLAB_HEADS)),
                          jnp.tile(wk[:, :, half:], (1, 1, SLAB_HEADS))], axis=2)
    wk = wk.reshape(d, kv_heads * MXU_TILE).astype(BF16)
    wv = kv_w[:, kv_heads * ATT_HD:].astype(BF16)
    cos, sin = _rope_tables(s, 1.0)
    rot_spec = pl.BlockSpec((ts, MXU_TILE // 2), lambda i, j: (j, 0))
    return pl.pallas_call(
        functools.partial(_kv_body, kv_heads=kv_heads),
        grid=(b, s // ts),
        in_specs=[pl.BlockSpec((1, ts, d), lambda i, j: (i, j, 0)), _resident((1, d)), _resident((1, MXU_TILE)),
                  rot_spec, rot_spec, _resident(wk.shape), _resident(wv.shape)],
        out_specs=[pl.BlockSpec((1, ts, kv_heads * MXU_TILE), lambda i, j: (i, j, 0)),
                   pl.BlockSpec((1, kv_heads * ATT_HD, ts), lambda i, j: (i, 0, j))],
        out_shape=[jax.ShapeDtypeStruct((b, s, kv_heads * MXU_TILE), BF16),
                   jax.ShapeDtypeStruct((b, kv_heads * ATT_HD, s), BF16)],
        compiler_params=pltpu.CompilerParams(
            dimension_semantics=("parallel", "parallel"), vmem_limit_bytes=VMEM_LIMIT_BYTES),
        name="shared_kv",
    )(x, kv_norm.reshape(1, d), _slab_gain(k_norm), cos, sin, wk, wv)


def _attn_body(x_ref, g_ref, rot_ref, ind_ref, bias_ref, sink_ref, kprev_ref, kcur_ref,
               vprev_ref, vcur_ref, wq_ref, wo_ref, o_ref, h_ref, lhs_ref, st_ref, att_ref):
    tq, d = x_ref.shape[1], x_ref.shape[2]
    n_slabs = d // MXU_TILE
    slabs_per_kv = ATT_GROUP // SLAB_HEADS
    n_kv = n_slabs // slabs_per_kv
    n_blk = tq // WINDOW
    x = x_ref[0]
    h_ref[...] = (x * _rms_scale(x) * g_ref[...]).astype(BF16)
    lane = lax.broadcasted_iota(jnp.int32, (WINDOW, MXU_TILE), 1)
    q_lanes = [(lane % (MXU_TILE // 2)) // (ATT_HD // 2) == j for j in range(SLAB_HEADS)]
    first = jnp.where(pl.program_id(1) == 0, 1, 0)

    def project(slab):
        return jnp.dot(h_ref[...], wq_ref[:, slab * MXU_TILE:(slab + 1) * MXU_TILE], preferred_element_type=F32)

    half = MXU_TILE // 2
    q_next = project(0)
    for slab in range(n_slabs):
        q = q_next
        if slab + 1 < n_slabs:
            q_next = project(slab + 1)
        ms = jnp.dot((q * q).astype(BF16), ind_ref[...], preferred_element_type=F32)
        q = q * lax.rsqrt(ms + NORM_EPS)
        q1, q2 = q[:, :half], q[:, half:]
        q = jnp.concatenate([q1 * rot_ref[0] - q2 * rot_ref[1], q2 * rot_ref[2] + q1 * rot_ref[3]], axis=1)
        q = q.astype(BF16)
        kv, sub = divmod(slab, slabs_per_kv)
        for blk in range(n_blk):
            rows = q[blk * WINDOW:(blk + 1) * WINDOW]
            for j in range(SLAB_HEADS):
                r0 = (sub * SLAB_HEADS + j) * WINDOW
                lhs_ref[blk, kv, r0:r0 + WINDOW, :] = jnp.where(q_lanes[j], rows, jnp.zeros_like(rows))

    cols = UNIT_HEADS * WINDOW
    parts = ATT_GROUP // UNIT_HEADS
    for blk in range(n_blk):
        for kv in range(n_kv):
            k_sl = slice(kv * MXU_TILE, (kv + 1) * MXU_TILE)
            if blk == 0:
                keys = jnp.concatenate([kprev_ref[0, :, k_sl], kcur_ref[0, :WINDOW, k_sl]], axis=0)
            else:
                keys = kcur_ref[0, (blk - 1) * WINDOW:(blk + 1) * WINDOW, k_sl]
            for part in range(parts):
                c_sl = slice(part * cols, (part + 1) * cols)
                st_ref[blk, kv, part] = lax.dot_general(
                    keys, lhs_ref[blk, kv, c_sl, :], NT_DIMS, preferred_element_type=F32)
    for blk in range(n_blk):
        for kv in range(n_kv):
            v_sl = slice(kv * ATT_HD, (kv + 1) * ATT_HD)
            if blk == 0:
                vals_t = jnp.concatenate([vprev_ref[0, v_sl, :], vcur_ref[0, v_sl, :WINDOW]], axis=1)
                bias = bias_ref[first]
            else:
                vals_t = vcur_ref[0, v_sl, (blk - 1) * WINDOW:(blk + 1) * WINDOW]
                bias = bias_ref[0]
            for part in range(parts):
                c_sl = slice(part * cols, (part + 1) * cols)
                st = st_ref[blk, kv, part] + bias
                sink = sink_ref[kv, :, c_sl]
                m = jnp.maximum(jnp.max(st, axis=0, keepdims=True), sink)
                p = jnp.exp2(st - m)
                denom = jnp.sum(p, axis=0, keepdims=True) + jnp.exp2(sink - m)
                ot = jnp.dot(vals_t, p.astype(BF16), preferred_element_type=F32) * (1.0 / denom)
                for j in range(UNIT_HEADS):
                    head = kv * ATT_GROUP + part * UNIT_HEADS + j
                    att_ref[head * ATT_HD:(head + 1) * ATT_HD, blk * WINDOW:(blk + 1) * WINDOW] = (
                        ot[:, j * WINDOW:(j + 1) * WINDOW].astype(BF16))
    y = lax.dot_general(att_ref[...], wo_ref[...], TN_DIMS, preferred_element_type=F32)
    o_ref[0] = x_ref[0] + y


def _attention_layer(x, g, k4, vt, w_q, q_norm, sinks, w_o, *, tq=512):
    b, s, d = x.shape
    heads = d // ATT_HD
    n_slabs = heads // SLAB_HEADS
    half = ATT_HD // 2
    nblk = tq // WINDOW
    assert s % tq == 0
    wq = w_q.reshape(d, n_slabs, SLAB_HEADS, 2, half).transpose(0, 1, 3, 2, 4).reshape(d, d).astype(BF16)
    cos, sin = _rope_tables(s, ATT_HD ** -0.5 * LOG2_E)
    g1, g2 = jnp.tile(q_norm[:half], SLAB_HEADS)[None, :], jnp.tile(q_norm[half:], SLAB_HEADS)[None, :]
    rot = jnp.stack([cos * g1, sin * g2, cos * g2, sin * g1])
    lane = np.arange(MXU_TILE)
    head_of = (lane % (MXU_TILE // 2)) // half
    ind = jnp.asarray((head_of[:, None] == head_of[None, :]) / ATT_HD, BF16)
    kj = np.arange(2 * WINDOW)[:, None]
    qi = np.tile(np.arange(WINDOW), UNIT_HEADS)[None, :]
    band = (kj > qi) & (kj <= qi + WINDOW)
    bias = jnp.asarray(np.stack([np.where(band, 0.0, NEG_INF), np.where(band & (kj >= WINDOW), 0.0, NEG_INF)]), F32)
    n_kv = heads // ATT_GROUP
    sink_rows = jnp.repeat(sinks.astype(F32).reshape(n_kv, ATT_GROUP) * LOG2_E, WINDOW, axis=1)[:, None, :]
    prev_blk = lambda j: jnp.maximum(j * nblk - 1, 0)
    return pl.pallas_call(
        _attn_body,
        grid=(b, s // tq),
        in_specs=[
            pl.BlockSpec((1, tq, d), lambda i, j: (i, j, 0)),
            _resident((1, d)),
            pl.BlockSpec((4, tq, MXU_TILE // 2), lambda i, j: (0, j, 0)),
            _resident(ind.shape), _resident(bias.shape), _resident(sink_rows.shape),
            pl.BlockSpec((1, WINDOW, k4.shape[2]), lambda i, j: (i, prev_blk(j), 0)),
            pl.BlockSpec((1, tq, k4.shape[2]), lambda i, j: (i, j, 0)),
            pl.BlockSpec((1, vt.shape[1], WINDOW), lambda i, j: (i, 0, prev_blk(j))),
            pl.BlockSpec((1, vt.shape[1], tq), lambda i, j: (i, 0, j)),
            _resident(wq.shape), _resident(w_o.shape),
        ],
        out_specs=pl.BlockSpec((1, tq, d), lambda i, j: (i, j, 0)),
        out_shape=jax.ShapeDtypeStruct((b, s, d), F32),
        scratch_shapes=[pltpu.VMEM((tq, d), BF16),
                        pltpu.VMEM((nblk, n_kv, ATT_GROUP * WINDOW, MXU_TILE), BF16),
                        pltpu.VMEM((nblk, n_kv, ATT_GROUP // UNIT_HEADS, 2 * WINDOW, UNIT_HEADS * WINDOW), F32),
                        pltpu.VMEM((d, tq), BF16)],
        compiler_params=pltpu.CompilerParams(
            dimension_semantics=("parallel", "parallel"), vmem_limit_bytes=VMEM_LIMIT_BYTES),
        name="swa_attention",
    )(x, g.reshape(1, d), rot, ind, bias, sink_rows, k4, k4, vt, vt, wq, w_o.astype(BF16))


def _ffn_layer(x, g, w_in, w_out):
    b, s, d = x.shape
    d_ff = w_out.shape[0]
    y = _ffn(x.reshape(b * s, d), g, w_in[:, :d_ff].astype(BF16), w_in[:, d_ff:].astype(BF16), w_out.astype(BF16))
    return y.reshape(b, s, d)


def kernel(x, ffn1_norm, ffn1_w_in, ffn1_w_out, mix_norm, ffn2_norm, ffn2_w_in, ffn2_w_out,
           ret_w_in, ret_w_out, kv_norm, kv_w, k_norm, attn_w_q, q_norm, attn_sinks, attn_w_o):
    depth = ffn1_norm.shape[0]
    n_ret = ret_w_in.shape[0]
    k4 = v4 = None
    for i in range(depth):
        x = _ffn_layer(x, ffn1_norm[i], ffn1_w_in[i], ffn1_w_out[i])
        if i < n_ret:
            x = _retention_layer(x, mix_norm[i], ret_w_in[i], ret_w_out[i])
        else:
            j = i - n_ret
            x = _attention_layer(x, mix_norm[i], k4, v4, attn_w_q[j], q_norm[j], attn_sinks[j], attn_w_o[j])
        x = _ffn_layer(x, ffn2_norm[i], ffn2_w_in[i], ffn2_w_out[i])
        if i == n_ret - 1:
            k4, v4 = _shared_kv(x, kv_norm, kv_w, k_norm)
    return x
```

```python
import functools

import jax
import jax.numpy as jnp
import numpy as np
from jax import lax
from jax.experimental import pallas as pl
from jax.experimental.pallas import tpu as pltpu

F32 = jnp.float32
BF16 = jnp.bfloat16

NORM_EPS = 1e-6
MXU_TILE = 256
VMEM_LIMIT_BYTES = 56 * 1024 * 1024


def _resident(shape):
    return pl.BlockSpec(shape, lambda *_: (0,) * len(shape), pipeline_mode=pl.Buffered(1))


def _rms_scale(x):
    return lax.rsqrt(jnp.mean(x * x, axis=-1, keepdims=True) + NORM_EPS)


FFN_NORM_PIECES = 8


def _ffn_body(x_ref, xnext_ref, g_ref, win_ref, wo_ref, o_ref, h_ref, a_ref, *, ff_chunk):
    tm = x_ref.shape[0]
    n_chunks = win_ref.shape[1] // (2 * ff_chunk)
    piece = tm // FFN_NORM_PIECES
    step = pl.program_id(0)
    cur, nxt = step % 2, (step + 1) % 2

    def norm_rows(src_ref, slot, rows):
        x = src_ref[rows, :]
        h_ref[slot, rows, :] = (x * _rms_scale(x) * g_ref[...]).astype(BF16)

    @pl.when(step == 0)
    def _():
        norm_rows(x_ref, 0, slice(0, tm))

    for c in range(n_chunks):
        sl = slice(c * ff_chunk, (c + 1) * ff_chunk)
        gu = jnp.dot(h_ref[cur], win_ref[:, 2 * c * ff_chunk:2 * (c + 1) * ff_chunk], preferred_element_type=F32)
        gate, up = gu[:, :ff_chunk], gu[:, ff_chunk:]
        a_ref[:, sl] = (gate * jax.nn.sigmoid(gate) * up).astype(BF16)
        if c < FFN_NORM_PIECES:
            norm_rows(xnext_ref, nxt, slice(c * piece, (c + 1) * piece))
    y = jnp.dot(a_ref[...], wo_ref[...], preferred_element_type=F32)
    o_ref[...] = x_ref[...] + 0.5 * y


def _ffn(x2d, g, w_in, wo, *, tm=1024, ff_chunk=MXU_TILE):
    t, d = x2d.shape
    d_ff = wo.shape[0]
    steps = t // tm
    assert t % tm == 0 and d_ff % ff_chunk == 0 and d_ff // ff_chunk >= FFN_NORM_PIECES
    w_in = w_in.reshape(d, 2, d_ff // ff_chunk, ff_chunk).transpose(0, 2, 1, 3).reshape(d, 2 * d_ff)
    return pl.pallas_call(
        functools.partial(_ffn_body, ff_chunk=ff_chunk),
        grid=(steps,),
        in_specs=[
            pl.BlockSpec((tm, d), lambda i: (i, 0)),
            pl.BlockSpec((tm, d), lambda i: (jnp.minimum(i + 1, steps - 1), 0)),
            _resident((1, d)),
            _resident((d, 2 * d_ff)),
            _resident((d_ff, d)),
        ],
        out_specs=pl.BlockSpec((tm, d), lambda i: (i, 0)),
        out_shape=jax.ShapeDtypeStruct((t, d), F32),
        scratch_shapes=[pltpu.VMEM((2, tm, d), BF16), pltpu.VMEM((tm, d_ff), BF16)],
        compiler_params=pltpu.CompilerParams(
            dimension_semantics=("arbitrary",), vmem_limit_bytes=VMEM_LIMIT_BYTES),
        name="ffn",
    )(x2d, x2d, g.reshape(1, d), w_in.astype(BF16), wo.astype(BF16))


RET_DK = 256
RET_DV_FACTOR = 2
RET_ROT_BASE = 10000.0
NT_DIMS = (((1,), (1,)), ((), ()))
TN_DIMS = (((0,), (0,)), ((), ()))


def _ret_gamma(h):
    return 1.0 - 2.0 ** (-5.0 - h)


def _rotate_halves(t, cos, sin):
    half = t.shape[1] // 2
    te, to = t[:, :half], t[:, half:]
    return jnp.concatenate([te * cos - to * sin, to * cos + te * sin], axis=1)


def _ret_body(x_ref, g_ref, cq_ref, sq_ref, ck_ref, sk_ref, xi_ref, zeta_ref, dmask_ref, win_ref, wo_ref,
              o_ref, h_ref, gated_ref, state_ref, *, heads, dk, dv, chunk):
    rows = x_ref.shape[1]

    @pl.when(pl.program_id(1) == 0)
    def _():
        state_ref[...] = jnp.zeros_like(state_ref)

    x = x_ref[0]
    h_ref[...] = (x * _rms_scale(x) * g_ref[...]).astype(BF16)
    k_off, v_off, g_off = heads * dk, 2 * heads * dk, 2 * heads * dk + heads * dv
    for hd in range(heads):
        proj = lambda off, width: jnp.dot(
            h_ref[...], win_ref[:, off + hd * width: off + (hd + 1) * width], preferred_element_type=F32)
        qr = _rotate_halves(proj(0, dk), cq_ref[...], sq_ref[...])
        kr = _rotate_halves(proj(k_off, dk), ck_ref[...], sk_ref[...])
        v = proj(v_off, dv).astype(BF16)
        gate = proj(g_off, dv)
        gate = gate * jax.nn.sigmoid(gate)
        state = state_ref[hd]
        for c in range(rows // chunk):
            sl = slice(c * chunk, (c + 1) * chunk)
            qc, kc, vc = qr[sl], kr[sl], v[sl]
            scores = lax.dot_general(qc.astype(BF16), kc.astype(BF16), NT_DIMS, preferred_element_type=F32)
            scores = (scores * dmask_ref[hd]).astype(BF16)
            out = (jnp.dot(scores, vc, preferred_element_type=F32)
                   + jnp.dot((qc * xi_ref[hd]).astype(BF16), state.astype(BF16), preferred_element_type=F32))
            kz = (kc * zeta_ref[hd]).astype(BF16)
            state = state * (_ret_gamma(hd) ** chunk) + lax.dot_general(kz, vc, TN_DIMS, preferred_element_type=F32)
            out = out * _rms_scale(out)
            gated_ref[sl, hd * dv:(hd + 1) * dv] = (out * gate[sl]).astype(BF16)
        state_ref[hd] = state
    y = jnp.dot(gated_ref[...], wo_ref[...], preferred_element_type=F32)
    o_ref[0] = x_ref[0] + y


def _ret_tables(s, chunk, heads, dk):
    half = dk // 2
    freq = 1.0 / (RET_ROT_BASE ** np.linspace(0.0, 1.0, half))
    ang = np.arange(s)[:, None] * freq[None, :]
    cos, sin = np.cos(ang), np.sin(ang)
    k_scale = dk ** -0.5
    gamma = np.array([_ret_gamma(h) for h in range(heads)])
    n = np.arange(chunk)
    diff = n[:, None] - n[None, :]
    dmask = np.where(diff[None] >= 0, gamma[:, None, None] ** np.maximum(diff, 0)[None], 0.0)
    xi = gamma[:, None] ** (n[None, :] + 1.0)
    zeta = gamma[:, None] ** (chunk - 1.0 - n[None, :])
    widen = lambda t: np.broadcast_to(t[:, :, None], (heads, chunk, dk))
    tables = (cos, sin, cos * k_scale, sin * k_scale, widen(xi), widen(zeta), dmask)
    return tuple(jnp.asarray(t, F32) for t in tables)


def _deinterleave_heads(w, heads, dk):
    d = w.shape[0]
    return w.reshape(d, heads, dk // 2, 2).transpose(0, 1, 3, 2).reshape(d, heads * dk)


def _retention_layer(x, g, w_in, w_out, *, chunk=MXU_TILE, ts=2 * MXU_TILE):
    b, s, d = x.shape
    heads = d // RET_DK
    dk, dv = RET_DK, RET_DV_FACTOR * d // heads
    assert s % ts == 0 and ts % chunk == 0
    dq = heads * dk
    win = jnp.concatenate([_deinterleave_heads(w_in[:, :dq], heads, dk),
                           _deinterleave_heads(w_in[:, dq:2 * dq], heads, dk),
                           w_in[:, 2 * dq:]], axis=1).astype(BF16)
    cq, sq, ck, sk, xi, zeta, dmask = _ret_tables(s, chunk, heads, dk)
    rot_spec = pl.BlockSpec((ts, dk // 2), lambda i, j: (j, 0))
    return pl.pallas_call(
        functools.partial(_ret_body, heads=heads, dk=dk, dv=dv, chunk=chunk),
        grid=(b, s // ts),
        in_specs=[
            pl.BlockSpec((1, ts, d), lambda i, j: (i, j, 0)),
            _resident((1, d)),
            rot_spec, rot_spec, rot_spec, rot_spec,
            _resident((heads, chunk, dk)),
            _resident((heads, chunk, dk)),
            _resident((heads, chunk, chunk)),
            _resident(win.shape),
            _resident(w_out.shape),
        ],
        out_specs=pl.BlockSpec((1, ts, d), lambda i, j: (i, j, 0)),
        out_shape=jax.ShapeDtypeStruct((b, s, d), F32),
        scratch_shapes=[pltpu.VMEM((ts, d), BF16), pltpu.VMEM((ts, heads * dv), BF16),
                        pltpu.VMEM((heads, dk, dv), F32)],
        compiler_params=pltpu.CompilerParams(
            dimension_semantics=("parallel", "arbitrary"), vmem_limit_bytes=VMEM_LIMIT_BYTES),
        name="retention",
    )(x, g.reshape(1, d), cq, sq, ck, sk, xi, zeta, dmask, win, w_out.astype(BF16))


ATT_HD = 64
ATT_GROUP = 8
SLAB_HEADS = MXU_TILE // ATT_HD
WINDOW = 128
ROPE_THETA = 10000.0
NEG_INF = -1e30
LOG2_E = 1.4426950408889634
UNIT_HEADS = 8
ONES_ROWS = 16


def _rope_tables(s, scale):
    half = ATT_HD // 2
    inv_freq = 1.0 / (ROPE_THETA ** (np.arange(0, ATT_HD, 2) / ATT_HD))
    ang = np.arange(s)[:, None] * inv_freq[None, :]
    return (jnp.asarray(np.tile(np.cos(ang), (1, SLAB_HEADS)) * scale, F32),
            jnp.asarray(np.tile(np.sin(ang), (1, SLAB_HEADS)) * scale, F32))


def _slab_gain(g):
    half = ATT_HD // 2
    return jnp.concatenate([jnp.tile(g[:half], SLAB_HEADS), jnp.tile(g[half:], SLAB_HEADS)]).reshape(1, MXU_TILE)


def _kv_body(x_ref, g_ref, kg_ref, cos_ref, sin_ref, wk_ref, wv_ref, k_ref, vt_ref, *, kv_heads):
    x = x_ref[0]
    h = (x * _rms_scale(x) * g_ref[...]).astype(BF16)
    for g in range(kv_heads):
        sl = slice(g * MXU_TILE, (g + 1) * MXU_TILE)
        k = jnp.dot(h, wk_ref[:, sl], preferred_element_type=F32)
        k = k * _rms_scale(k) * kg_ref[...]
        k_ref[0, :, sl] = _rotate_halves(k, cos_ref[...], sin_ref[...]).astype(BF16)
    v = jnp.dot(h, wv_ref[...], preferred_element_type=F32)
    vt_ref[0] = v.T.astype(BF16)


def _shared_kv(x, kv_norm, kv_w, k_norm, *, ts=1024):
    b, s, d = x.shape
    kv_heads = kv_w.shape[1] // (2 * ATT_HD)
    half = ATT_HD // 2
    wk = kv_w[:, :kv_heads * ATT_HD].reshape(d, kv_heads, ATT_HD)
    wk = jnp.concatenate([jnp.tile(wk[:, :, :half], (1, 1, SLAB_HEADS)),
                          jnp.tile(wk[:, :, half:], (1, 1, SLAB_HEADS))], axis=2)
    wk = wk.reshape(d, kv_heads * MXU_TILE).astype(BF16)
    wv = kv_w[:, kv_heads * ATT_HD:].astype(BF16)
    cos, sin = _rope_tables(s, 1.0)
    rot_spec = pl.BlockSpec((ts, MXU_TILE // 2), lambda i, j: (j, 0))
    return pl.pallas_call(
        functools.partial(_kv_body, kv_heads=kv_heads),
        grid=(b, s // ts),
        in_specs=[pl.BlockSpec((1, ts, d), lambda i, j: (i, j, 0)), _resident((1, d)), _resident((1, MXU_TILE)),
                  rot_spec, rot_spec, _resident(wk.shape), _resident(wv.shape)],
        out_specs=[pl.BlockSpec((1, ts, kv_heads * MXU_TILE), lambda i, j: (i, j, 0)),
                   pl.BlockSpec((1, kv_heads * ATT_HD, ts), lambda i, j: (i, 0, j))],
        out_shape=[jax.ShapeDtypeStruct((b, s, kv_heads * MXU_TILE), BF16),
                   jax.ShapeDtypeStruct((b, kv_heads * ATT_HD, s), BF16)],
        compiler_params=pltpu.CompilerParams(
            dimension_semantics=("parallel", "parallel"), vmem_limit_bytes=VMEM_LIMIT_BYTES),
        name="shared_kv",
    )(x, kv_norm.reshape(1, d), _slab_gain(k_norm), cos, sin, wk, wv)


def _attn_body(x_ref, g_ref, rot_ref, ind_ref, bias_ref, sink_ref, kprev_ref, kcur_ref,
               vprev_ref, vcur_ref, wq_ref, wo_ref, o_ref, h_ref, lhs_ref, st_ref, max_ref, p_ref, att_ref):
    tq, d = x_ref.shape[1], x_ref.shape[2]
    n_slabs = d // MXU_TILE
    slabs_per_kv = ATT_GROUP // SLAB_HEADS
    n_kv = n_slabs // slabs_per_kv
    n_blk = tq // WINDOW
    x = x_ref[0]
    h_ref[...] = (x * _rms_scale(x) * g_ref[...]).astype(BF16)
    lane = lax.broadcasted_iota(jnp.int32, (WINDOW, MXU_TILE), 1)
    q_lanes = [(lane % (MXU_TILE // 2)) // (ATT_HD // 2) == j for j in range(SLAB_HEADS)]
    first = jnp.where(pl.program_id(1) == 0, 1, 0)

    def project(slab):
        return jnp.dot(h_ref[...], wq_ref[:, slab * MXU_TILE:(slab + 1) * MXU_TILE], preferred_element_type=F32)

    half = MXU_TILE // 2
    q_next = project(0)
    for slab in range(n_slabs):
        q = q_next
        if slab + 1 < n_slabs:
            q_next = project(slab + 1)
        ms = jnp.dot((q * q).astype(BF16), ind_ref[...], preferred_element_type=F32)
        q = q * lax.rsqrt(ms + NORM_EPS)
        q1, q2 = q[:, :half], q[:, half:]
        q = jnp.concatenate([q1 * rot_ref[0] - q2 * rot_ref[1], q2 * rot_ref[2] + q1 * rot_ref[3]], axis=1)
        q = q.astype(BF16)
        kv, sub = divmod(slab, slabs_per_kv)
        for blk in range(n_blk):
            rows = q[blk * WINDOW:(blk + 1) * WINDOW]
            for j in range(SLAB_HEADS):
                r0 = (sub * SLAB_HEADS + j) * WINDOW
                lhs_ref[blk, kv, r0:r0 + WINDOW, :] = jnp.where(q_lanes[j], rows, jnp.zeros_like(rows))

    cols = UNIT_HEADS * WINDOW
    parts = ATT_GROUP // UNIT_HEADS
    for blk in range(n_blk):
        for kv in range(n_kv):
            k_sl = slice(kv * MXU_TILE, (kv + 1) * MXU_TILE)
            if blk == 0:
                keys = jnp.concatenate([kprev_ref[0, :, k_sl], kcur_ref[0, :WINDOW, k_sl]], axis=0)
            else:
                keys = kcur_ref[0, (blk - 1) * WINDOW:(blk + 1) * WINDOW, k_sl]
            for part in range(parts):
                c_sl = slice(part * cols, (part + 1) * cols)
                raw = lax.dot_general(keys, lhs_ref[blk, kv, c_sl, :], NT_DIMS, preferred_element_type=F32)
                for j in range(UNIT_HEADS):
                    h_sl = slice(j * WINDOW, (j + 1) * WINDOW)
                    s = raw[:, h_sl] + (bias_ref[first] if blk == 0 else bias_ref[0])
                    st_ref[blk, kv, part, :, h_sl] = s
                    max_ref[blk, kv, part, :, h_sl] = jnp.max(s, axis=0, keepdims=True)
    unit = 0
    for blk in range(n_blk):
        for kv in range(n_kv):
            v_sl = slice(kv * ATT_HD, (kv + 1) * ATT_HD)
            if blk == 0:
                vals_t = jnp.concatenate([vprev_ref[0, v_sl, :], vcur_ref[0, v_sl, :WINDOW]], axis=1)
            else:
                vals_t = vcur_ref[0, v_sl, (blk - 1) * WINDOW:(blk + 1) * WINDOW]
            vals_t = jnp.concatenate([vals_t, jnp.ones((ONES_ROWS, 2 * WINDOW), BF16)], axis=0)
            for part in range(parts):
                slot = unit % 2
                unit += 1
                sink = sink_ref[kv, :, part * cols:(part + 1) * cols]
                m = jnp.maximum(max_ref[blk, kv, part], sink)
                for j in range(UNIT_HEADS):
                    h_sl = slice(j * WINDOW, (j + 1) * WINDOW)
                    p_ref[slot, :, h_sl] = jnp.exp2(st_ref[blk, kv, part, :, h_sl] - m[:, h_sl]).astype(BF16)
                ot = jnp.dot(vals_t, p_ref[slot], preferred_element_type=F32)
                inv = 1.0 / (ot[ATT_HD:ATT_HD + 1, :] + jnp.exp2(sink - m))
                for j in range(UNIT_HEADS):
                    h_sl = slice(j * WINDOW, (j + 1) * WINDOW)
                    head = kv * ATT_GROUP + part * UNIT_HEADS + j
                    att_ref[head * ATT_HD:(head + 1) * ATT_HD, blk * WINDOW:(blk + 1) * WINDOW] = (
                        ot[:ATT_HD, h_sl] * inv[:, h_sl]).astype(BF16)
    y = lax.dot_general(att_ref[...], wo_ref[...], TN_DIMS, preferred_element_type=F32)
    o_ref[0] = x_ref[0] + y


def _attention_layer(x, g, k4, vt, w_q, q_norm, sinks, w_o, *, tq=512):
    b, s, d = x.shape
    heads = d // ATT_HD
    n_slabs = heads // SLAB_HEADS
    half = ATT_HD // 2
    nblk = tq // WINDOW
    assert s % tq == 0
    wq = w_q.reshape(d, n_slabs, SLAB_HEADS, 2, half).transpose(0, 1, 3, 2, 4).reshape(d, d).astype(BF16)
    cos, sin = _rope_tables(s, ATT_HD ** -0.5 * LOG2_E)
    g1, g2 = jnp.tile(q_norm[:half], SLAB_HEADS)[None, :], jnp.tile(q_norm[half:], SLAB_HEADS)[None, :]
    rot = jnp.stack([cos * g1, sin * g2, cos * g2, sin * g1])
    lane = np.arange(MXU_TILE)
    head_of = (lane % (MXU_TILE // 2)) // half
    ind = jnp.asarray((head_of[:, None] == head_of[None, :]) / ATT_HD, BF16)
    kj = np.arange(2 * WINDOW)[:, None]
    qi = np.arange(WINDOW)[None, :]
    band = (kj > qi) & (kj <= qi + WINDOW)
    bias = jnp.asarray(np.stack([np.where(band, 0.0, NEG_INF), np.where(band & (kj >= WINDOW), 0.0, NEG_INF)]), F32)
    n_kv = heads // ATT_GROUP
    sink_rows = jnp.repeat(sinks.astype(F32).reshape(n_kv, ATT_GROUP) * LOG2_E, WINDOW, axis=1)[:, None, :]
    prev_blk = lambda j: jnp.maximum(j * nblk - 1, 0)
    return pl.pallas_call(
        _attn_body,
        grid=(b, s // tq),
        in_specs=[
            pl.BlockSpec((1, tq, d), lambda i, j: (i, j, 0)),
            _resident((1, d)),
            pl.BlockSpec((4, tq, MXU_TILE // 2), lambda i, j: (0, j, 0)),
            _resident(ind.shape), _resident(bias.shape), _resident(sink_rows.shape),
            pl.BlockSpec((1, WINDOW, k4.shape[2]), lambda i, j: (i, prev_blk(j), 0)),
            pl.BlockSpec((1, tq, k4.shape[2]), lambda i, j: (i, j, 0)),
            pl.BlockSpec((1, vt.shape[1], WINDOW), lambda i, j: (i, 0, prev_blk(j))),
            pl.BlockSpec((1, vt.shape[1], tq), lambda i, j: (i, 0, j)),
            _resident(wq.shape), _resident(w_o.shape),
        ],
        out_specs=pl.BlockSpec((1, tq, d), lambda i, j: (i, j, 0)),
        out_shape=jax.ShapeDtypeStruct((b, s, d), F32),
        scratch_shapes=[pltpu.VMEM((tq, d), BF16),
                        pltpu.VMEM((nblk, n_kv, ATT_GROUP * WINDOW, MXU_TILE), BF16),
                        pltpu.VMEM((nblk, n_kv, ATT_GROUP // UNIT_HEADS, 2 * WINDOW, UNIT_HEADS * WINDOW), F32),
                        pltpu.VMEM((nblk, n_kv, ATT_GROUP // UNIT_HEADS, 1, UNIT_HEADS * WINDOW), F32),
                        pltpu.VMEM((2, 2 * WINDOW, UNIT_HEADS * WINDOW), BF16),
                        pltpu.VMEM((d, tq), BF16)],
        compiler_params=pltpu.CompilerParams(
            dimension_semantics=("parallel", "parallel"), vmem_limit_bytes=VMEM_LIMIT_BYTES),
        name="swa_attention",
    )(x, g.reshape(1, d), rot, ind, bias, sink_rows, k4, k4, vt, vt, wq, w_o.astype(BF16))


def _ffn_layer(x, g, w_in, w_out):
    b, s, d = x.shape
    d_ff = w_out.shape[0]
    y = _ffn(x.reshape(b * s, d), g, w_in, w_out)
    return y.reshape(b, s, d)


def kernel(x, ffn1_norm, ffn1_w_in, ffn1_w_out, mix_norm, ffn2_norm, ffn2_w_in, ffn2_w_out,
           ret_w_in, ret_w_out, kv_norm, kv_w, k_norm, attn_w_q, q_norm, attn_sinks, attn_w_o):
    depth = ffn1_norm.shape[0]
    n_ret = ret_w_in.shape[0]
    k4 = v4 = None
    for i in range(depth):
        x = _ffn_layer(x, ffn1_norm[i], ffn1_w_in[i], ffn1_w_out[i])
        if i < n_ret:
            x = _retention_layer(x, mix_norm[i], ret_w_in[i], ret_w_out[i])
        else:
            j = i - n_ret
            x = _attention_layer(x, mix_norm[i], k4, v4, attn_w_q[j], q_norm[j], attn_sinks[j], attn_w_o[j])
        x = _ffn_layer(x, ffn2_norm[i], ffn2_w_in[i], ffn2_w_out[i])
        if i == n_ret - 1:
            k4, v4 = _shared_kv(x, kv_norm, kv_w, k_norm)
    return x
```

```python
import functools

import jax
import jax.numpy as jnp
import numpy as np
from jax import lax
from jax.experimental import pallas as pl
from jax.experimental.pallas import tpu as pltpu

F32 = jnp.float32
BF16 = jnp.bfloat16

NORM_EPS = 1e-6
MXU_TILE = 256
VMEM_LIMIT_BYTES = 56 * 1024 * 1024


def _resident(shape):
    return pl.BlockSpec(shape, lambda *_: (0,) * len(shape), pipeline_mode=pl.Buffered(1))


def _rms_scale(x):
    return lax.rsqrt(jnp.mean(x * x, axis=-1, keepdims=True) + NORM_EPS)


FFN_NORM_PIECES = 8


def _ffn_body(x_ref, xnext_ref, g_ref, win_ref, wo_ref, o_ref, h_ref, a_ref, *, ff_chunk):
    tm = x_ref.shape[0]
    d_ff = wo_ref.shape[0]
    n_chunks = d_ff // ff_chunk
    piece = tm // FFN_NORM_PIECES
    step = pl.program_id(0)
    cur, nxt = step % 2, (step + 1) % 2

    def norm_rows(src_ref, slot, rows):
        x = src_ref[rows, :]
        h_ref[slot, rows, :] = (x * _rms_scale(x) * g_ref[...]).astype(BF16)

    @pl.when(step == 0)
    def _():
        norm_rows(x_ref, 0, slice(0, tm))

    for c in range(n_chunks):
        sl = slice(c * ff_chunk, (c + 1) * ff_chunk)
        gate = jnp.dot(h_ref[cur], win_ref[:, sl], preferred_element_type=F32)
        up = jnp.dot(h_ref[cur], win_ref[:, d_ff + c * ff_chunk:d_ff + (c + 1) * ff_chunk],
                     preferred_element_type=F32)
        a_ref[:, sl] = (gate * jax.nn.sigmoid(gate) * up).astype(BF16)
        if c < FFN_NORM_PIECES:
            norm_rows(xnext_ref, nxt, slice(c * piece, (c + 1) * piece))
    y = jnp.dot(a_ref[...], wo_ref[...], preferred_element_type=F32)
    o_ref[...] = x_ref[...] + 0.5 * y


def _ffn(x2d, g, w_in, wo, *, tm=1024, ff_chunk=MXU_TILE):
    t, d = x2d.shape
    d_ff = wo.shape[0]
    steps = t // tm
    assert t % tm == 0 and d_ff % ff_chunk == 0 and d_ff // ff_chunk >= FFN_NORM_PIECES
    return pl.pallas_call(
        functools.partial(_ffn_body, ff_chunk=ff_chunk),
        grid=(steps,),
        in_specs=[
            pl.BlockSpec((tm, d), lambda i: (i, 0)),
            pl.BlockSpec((tm, d), lambda i: (jnp.minimum(i + 1, steps - 1), 0)),
            _resident((1, d)),
            _resident((d, 2 * d_ff)),
            _resident((d_ff, d)),
        ],
        out_specs=pl.BlockSpec((tm, d), lambda i: (i, 0)),
        out_shape=jax.ShapeDtypeStruct((t, d), F32),
        scratch_shapes=[pltpu.VMEM((2, tm, d), BF16), pltpu.VMEM((tm, d_ff), BF16)],
        compiler_params=pltpu.CompilerParams(
            dimension_semantics=("arbitrary",), vmem_limit_bytes=VMEM_LIMIT_BYTES),
        name="ffn",
    )(x2d, x2d, g.reshape(1, d), w_in.astype(BF16), wo.astype(BF16))


RET_DK = 256
RET_DV_FACTOR = 2
RET_ROT_BASE = 10000.0
NT_DIMS = (((1,), (1,)), ((), ()))
TN_DIMS = (((0,), (0,)), ((), ()))


def _ret_gamma(h):
    return 1.0 - 2.0 ** (-5.0 - h)


def _rotate_halves(t, cos, sin):
    half = t.shape[1] // 2
    te, to = t[:, :half], t[:, half:]
    return jnp.concatenate([te * cos - to * sin, to * cos + te * sin], axis=1)


def _ret_body(x_ref, g_ref, cq_ref, sq_ref, ck_ref, sk_ref, xi_ref, zeta_ref, dmask_ref, win_ref, wo_ref,
              o_ref, h_ref, gated_ref, state_ref, *, heads, dk, dv, chunk):
    rows = x_ref.shape[1]

    @pl.when(pl.program_id(1) == 0)
    def _():
        state_ref[...] = jnp.zeros_like(state_ref)

    x = x_ref[0]
    h_ref[...] = (x * _rms_scale(x) * g_ref[...]).astype(BF16)
    k_off, v_off, g_off = heads * dk, 2 * heads * dk, 2 * heads * dk + heads * dv
    for hd in range(heads):
        proj = lambda off, width: jnp.dot(
            h_ref[...], win_ref[:, off + hd * width: off + (hd + 1) * width], preferred_element_type=F32)
        qr = _rotate_halves(proj(0, dk), cq_ref[...], sq_ref[...])
        kr = _rotate_halves(proj(k_off, dk), ck_ref[...], sk_ref[...])
        v = proj(v_off, dv).astype(BF16)
        gate = proj(g_off, dv)
        gate = gate * jax.nn.sigmoid(gate)
        state = state_ref[hd]
        for c in range(rows // chunk):
            sl = slice(c * chunk, (c + 1) * chunk)
            qc, kc, vc = qr[sl], kr[sl], v[sl]
            scores = lax.dot_general(qc.astype(BF16), kc.astype(BF16), NT_DIMS, preferred_element_type=F32)
            scores = (scores * dmask_ref[hd]).astype(BF16)
            out = (jnp.dot(scores, vc, preferred_element_type=F32)
                   + jnp.dot((qc * xi_ref[hd]).astype(BF16), state.astype(BF16), preferred_element_type=F32))
            kz = (kc * zeta_ref[hd]).astype(BF16)
            state = state * (_ret_gamma(hd) ** chunk) + lax.dot_general(kz, vc, TN_DIMS, preferred_element_type=F32)
            out = out * _rms_scale(out)
            gated_ref[hd, sl, :] = (out * gate[sl]).astype(BF16)
        state_ref[hd] = state
    gated = jnp.concatenate([gated_ref[hd] for hd in range(heads)], axis=1)
    y = jnp.dot(gated, wo_ref[...], preferred_element_type=F32)
    o_ref[0] = x_ref[0] + y


def _ret_tables(s, chunk, heads, dk):
    half = dk // 2
    freq = 1.0 / (RET_ROT_BASE ** np.linspace(0.0, 1.0, half))
    ang = np.arange(s)[:, None] * freq[None, :]
    cos, sin = np.cos(ang), np.sin(ang)
    k_scale = dk ** -0.5
    gamma = np.array([_ret_gamma(h) for h in range(heads)])
    n = np.arange(chunk)
    diff = n[:, None] - n[None, :]
    dmask = np.where(diff[None] >= 0, gamma[:, None, None] ** np.maximum(diff, 0)[None], 0.0)
    xi = gamma[:, None] ** (n[None, :] + 1.0)
    zeta = gamma[:, None] ** (chunk - 1.0 - n[None, :])
    widen = lambda t: np.broadcast_to(t[:, :, None], (heads, chunk, dk))
    tables = (cos, sin, cos * k_scale, sin * k_scale, widen(xi), widen(zeta), dmask)
    return tuple(jnp.asarray(t, F32) for t in tables)


def _deinterleave_heads(w, heads, dk):
    d = w.shape[0]
    return w.reshape(d, heads, dk // 2, 2).transpose(0, 1, 3, 2).reshape(d, heads * dk)


def _retention_layer(x, g, w_in, w_out, *, chunk=MXU_TILE, ts=2 * MXU_TILE):
    b, s, d = x.shape
    heads = d // RET_DK
    dk, dv = RET_DK, RET_DV_FACTOR * d // heads
    assert s % ts == 0 and ts % chunk == 0
    dq = heads * dk
    win = jnp.concatenate([_deinterleave_heads(w_in[:, :dq], heads, dk),
                           _deinterleave_heads(w_in[:, dq:2 * dq], heads, dk),
                           w_in[:, 2 * dq:]], axis=1).astype(BF16)
    cq, sq, ck, sk, xi, zeta, dmask = _ret_tables(s, chunk, heads, dk)
    rot_spec = pl.BlockSpec((ts, dk // 2), lambda i, j: (j, 0))
    return pl.pallas_call(
        functools.partial(_ret_body, heads=heads, dk=dk, dv=dv, chunk=chunk),
        grid=(b, s // ts),
        in_specs=[
            pl.BlockSpec((1, ts, d), lambda i, j: (i, j, 0)),
            _resident((1, d)),
            rot_spec, rot_spec, rot_spec, rot_spec,
            _resident((heads, chunk, dk)),
            _resident((heads, chunk, dk)),
            _resident((heads, chunk, chunk)),
            _resident(win.shape),
            _resident(w_out.shape),
        ],
        out_specs=pl.BlockSpec((1, ts, d), lambda i, j: (i, j, 0)),
        out_shape=jax.ShapeDtypeStruct((b, s, d), F32),
        scratch_shapes=[pltpu.VMEM((ts, d), BF16), pltpu.VMEM((heads, ts, dv), BF16),
                        pltpu.VMEM((heads, dk, dv), F32)],
        compiler_params=pltpu.CompilerParams(
            dimension_semantics=("parallel", "arbitrary"), vmem_limit_bytes=VMEM_LIMIT_BYTES),
        name="retention",
    )(x, g.reshape(1, d), cq, sq, ck, sk, xi, zeta, dmask, win, w_out.astype(BF16))


ATT_HD = 64
ATT_GROUP = 8
SLAB_HEADS = MXU_TILE // ATT_HD
WINDOW = 128
ROPE_THETA = 10000.0
NEG_INF = -1e30
LOG2_E = 1.4426950408889634
UNIT_HEADS = 8
ONES_ROWS = 16


def _rope_tables(s, scale):
    half = ATT_HD // 2
    inv_freq = 1.0 / (ROPE_THETA ** (np.arange(0, ATT_HD, 2) / ATT_HD))
    ang = np.arange(s)[:, None] * inv_freq[None, :]
    return (jnp.asarray(np.tile(np.cos(ang), (1, SLAB_HEADS)) * scale, F32),
            jnp.asarray(np.tile(np.sin(ang), (1, SLAB_HEADS)) * scale, F32))


def _slab_gain(g):
    half = ATT_HD // 2
    return jnp.concatenate([jnp.tile(g[:half], SLAB_HEADS), jnp.tile(g[half:], SLAB_HEADS)]).reshape(1, MXU_TILE)


def _kv_body(x_ref, g_ref, kg_ref, cos_ref, sin_ref, wk_ref, wv_ref, k_ref, vt_ref, *, kv_heads):
    x = x_ref[0]
    h = (x * _rms_scale(x) * g_ref[...]).astype(BF16)
    for g in range(kv_heads):
        sl = slice(g * MXU_TILE, (g + 1) * MXU_TILE)
        k = jnp.dot(h, wk_ref[:, sl], preferred_element_type=F32)
        k = k * _rms_scale(k) * kg_ref[...]
        k_ref[0, :, sl] = _rotate_halves(k, cos_ref[...], sin_ref[...]).astype(BF16)
    v = jnp.dot(h, wv_ref[...], preferred_element_type=F32)
    vt_ref[0] = v.T.astype(BF16)


def _shared_kv(x, kv_norm, kv_w, k_norm, *, ts=1024):
    b, s, d = x.shape
    kv_heads = kv_w.shape[1] // (2 * ATT_HD)
    half = ATT_HD // 2
    wk = kv_w[:, :kv_heads * ATT_HD].reshape(d, kv_heads, ATT_HD)
    wk = jnp.concatenate([jnp.tile(wk[:, :, :half], (1, 1, SLAB_HEADS)),
                          jnp.tile(wk[:, :, half:], (1, 1, SLAB_HEADS))], axis=2)
    wk = wk.reshape(d, kv_heads * MXU_TILE).astype(BF16)
    wv = kv_w[:, kv_heads * ATT_HD:].astype(BF16)
    cos, sin = _rope_tables(s, 1.0)
    rot_spec = pl.BlockSpec((ts, MXU_TILE // 2), lambda i, j: (j, 0))
    return pl.pallas_call(
        functools.partial(_kv_body, kv_heads=kv_heads),
        grid=(b, s // ts),
        in_specs=[pl.BlockSpec((1, ts, d), lambda i, j: (i, j, 0)), _resident((1, d)), _resident((1, MXU_TILE)),
                  rot_spec, rot_spec, _resident(wk.shape), _resident(wv.shape)],
        out_specs=[pl.BlockSpec((1, ts, kv_heads * MXU_TILE), lambda i, j: (i, j, 0)),
                   pl.BlockSpec((1, kv_heads * ATT_HD, ts), lambda i, j: (i, 0, j))],
        out_shape=[jax.ShapeDtypeStruct((b, s, kv_heads * MXU_TILE), BF16),
                   jax.ShapeDtypeStruct((b, kv_heads * ATT_HD, s), BF16)],
        compiler_params=pltpu.CompilerParams(
            dimension_semantics=("parallel", "parallel"), vmem_limit_bytes=VMEM_LIMIT_BYTES),
        name="shared_kv",
    )(x, kv_norm.reshape(1, d), _slab_gain(k_norm), cos, sin, wk, wv)


def _attn_body(x_ref, g_ref, rot_ref, ind_ref, bias_ref, sink_ref, kprev_ref, kcur_ref,
               vprev_ref, vcur_ref, wq_ref, wo_ref, o_ref, h_ref, lhs_ref, st_ref, max_ref, p_ref, att_ref):
    tq, d = x_ref.shape[1], x_ref.shape[2]
    n_slabs = d // MXU_TILE
    slabs_per_kv = ATT_GROUP // SLAB_HEADS
    n_kv = n_slabs // slabs_per_kv
    n_blk = tq // WINDOW
    x = x_ref[0]
    h_ref[...] = (x * _rms_scale(x) * g_ref[...]).astype(BF16)
    lane = lax.broadcasted_iota(jnp.int32, (WINDOW, MXU_TILE), 1)
    q_lanes = [(lane % (MXU_TILE // 2)) // (ATT_HD // 2) == j for j in range(SLAB_HEADS)]
    first = jnp.where(pl.program_id(1) == 0, 1, 0)

    def project(slab):
        return jnp.dot(h_ref[...], wq_ref[:, slab * MXU_TILE:(slab + 1) * MXU_TILE], preferred_element_type=F32)

    half = MXU_TILE // 2
    q_next = project(0)
    for slab in range(n_slabs):
        q = q_next
        if slab + 1 < n_slabs:
            q_next = project(slab + 1)
        ms = jnp.dot((q * q).astype(BF16), ind_ref[...], preferred_element_type=F32)
        q = q * lax.rsqrt(ms + NORM_EPS)
        q1, q2 = q[:, :half], q[:, half:]
        q = jnp.concatenate([q1 * rot_ref[0] - q2 * rot_ref[1], q2 * rot_ref[2] + q1 * rot_ref[3]], axis=1)
        q = q.astype(BF16)
        kv, sub = divmod(slab, slabs_per_kv)
        for blk in range(n_blk):
            rows = q[blk * WINDOW:(blk + 1) * WINDOW]
            for j in range(SLAB_HEADS):
                r0 = (sub * SLAB_HEADS + j) * WINDOW
                lhs_ref[blk, kv, r0:r0 + WINDOW, :] = jnp.where(q_lanes[j], rows, jnp.zeros_like(rows))

    cols = UNIT_HEADS * WINDOW
    parts = ATT_GROUP // UNIT_HEADS
    for blk in range(n_blk):
        for kv in range(n_kv):
            k_sl = slice(kv * MXU_TILE, (kv + 1) * MXU_TILE)
            if blk == 0:
                keys = jnp.concatenate([kprev_ref[0, :, k_sl], kcur_ref[0, :WINDOW, k_sl]], axis=0)
            else:
                keys = kcur_ref[0, (blk - 1) * WINDOW:(blk + 1) * WINDOW, k_sl]
            for part in range(parts):
                c_sl = slice(part * cols, (part + 1) * cols)
                raw = lax.dot_general(keys, lhs_ref[blk, kv, c_sl, :], NT_DIMS, preferred_element_type=F32)
                for j in range(UNIT_HEADS):
                    h_sl = slice(j * WINDOW, (j + 1) * WINDOW)
                    s = raw[:, h_sl] + (bias_ref[first] if blk == 0 else bias_ref[0])
                    st_ref[blk, kv, part, j] = s
                    max_ref[blk, kv, part, :, h_sl] = jnp.max(s, axis=0, keepdims=True)
    unit = 0
    for blk in range(n_blk):
        for kv in range(n_kv):
            v_sl = slice(kv * ATT_HD, (kv + 1) * ATT_HD)
            if blk == 0:
                vals_t = jnp.concatenate([vprev_ref[0, v_sl, :], vcur_ref[0, v_sl, :WINDOW]], axis=1)
            else:
                vals_t = vcur_ref[0, v_sl, (blk - 1) * WINDOW:(blk + 1) * WINDOW]
            vals_t = jnp.concatenate([vals_t, jnp.ones((ONES_ROWS, 2 * WINDOW), BF16)], axis=0)
            for part in range(parts):
                slot = unit % 2
                unit += 1
                sink = sink_ref[kv, :, part * cols:(part + 1) * cols]
                m = jnp.maximum(max_ref[blk, kv, part], sink)
                for j in range(UNIT_HEADS):
                    h_sl = slice(j * WINDOW, (j + 1) * WINDOW)
                    p_ref[slot, j] = jnp.exp2(st_ref[blk, kv, part, j] - m[:, h_sl]).astype(BF16)
                p_t = jnp.concatenate([p_ref[slot, j] for j in range(UNIT_HEADS)], axis=1)
                ot = jnp.dot(vals_t, p_t, preferred_element_type=F32)
                inv = 1.0 / (ot[ATT_HD:ATT_HD + 1, :] + jnp.exp2(sink - m))
                for j in range(UNIT_HEADS):
                    h_sl = slice(j * WINDOW, (j + 1) * WINDOW)
                    head = kv * ATT_GROUP + part * UNIT_HEADS + j
                    att_ref[blk, head * ATT_HD:(head + 1) * ATT_HD, :] = (
                        ot[:ATT_HD, h_sl] * inv[:, h_sl]).astype(BF16)
    att_t = jnp.concatenate([att_ref[blk] for blk in range(n_blk)], axis=1)
    y = lax.dot_general(att_t, wo_ref[...], TN_DIMS, preferred_element_type=F32)
    o_ref[0] = x_ref[0] + y


def _attention_layer(x, g, k4, vt, w_q, q_norm, sinks, w_o, *, tq=512):
    b, s, d = x.shape
    heads = d // ATT_HD
    n_slabs = heads // SLAB_HEADS
    half = ATT_HD // 2
    nblk = tq // WINDOW
    assert s % tq == 0
    wq = w_q.reshape(d, n_slabs, SLAB_HEADS, 2, half).transpose(0, 1, 3, 2, 4).reshape(d, d).astype(BF16)
    cos, sin = _rope_tables(s, ATT_HD ** -0.5 * LOG2_E)
    g1, g2 = jnp.tile(q_norm[:half], SLAB_HEADS)[None, :], jnp.tile(q_norm[half:], SLAB_HEADS)[None, :]
    rot = jnp.stack([cos * g1, sin * g2, cos * g2, sin * g1])
    lane = np.arange(MXU_TILE)
    head_of = (lane % (MXU_TILE // 2)) // half
    ind = jnp.asarray((head_of[:, None] == head_of[None, :]) / ATT_HD, BF16)
    kj = np.arange(2 * WINDOW)[:, None]
    qi = np.arange(WINDOW)[None, :]
    band = (kj > qi) & (kj <= qi + WINDOW)
    bias = jnp.asarray(np.stack([np.where(band, 0.0, NEG_INF), np.where(band & (kj >= WINDOW), 0.0, NEG_INF)]), F32)
    n_kv = heads // ATT_GROUP
    sink_rows = jnp.repeat(sinks.astype(F32).reshape(n_kv, ATT_GROUP) * LOG2_E, WINDOW, axis=1)[:, None, :]
    prev_blk = lambda j: jnp.maximum(j * nblk - 1, 0)
    return pl.pallas_call(
        _attn_body,
        grid=(b, s // tq),
        in_specs=[
            pl.BlockSpec((1, tq, d), lambda i, j: (i, j, 0)),
            _resident((1, d)),
            pl.BlockSpec((4, tq, MXU_TILE // 2), lambda i, j: (0, j, 0)),
            _resident(ind.shape), _resident(bias.shape), _resident(sink_rows.shape),
            pl.BlockSpec((1, WINDOW, k4.shape[2]), lambda i, j: (i, prev_blk(j), 0)),
            pl.BlockSpec((1, tq, k4.shape[2]), lambda i, j: (i, j, 0)),
            pl.BlockSpec((1, vt.shape[1], WINDOW), lambda i, j: (i, 0, prev_blk(j))),
            pl.BlockSpec((1, vt.shape[1], tq), lambda i, j: (i, 0, j)),
            _resident(wq.shape), _resident(w_o.shape),
        ],
        out_specs=pl.BlockSpec((1, tq, d), lambda i, j: (i, j, 0)),
        out_shape=jax.ShapeDtypeStruct((b, s, d), F32),
        scratch_shapes=[pltpu.VMEM((tq, d), BF16),
                        pltpu.VMEM((nblk, n_kv, ATT_GROUP * WINDOW, MXU_TILE), BF16),
                        pltpu.VMEM((nblk, n_kv, ATT_GROUP // UNIT_HEADS, UNIT_HEADS, 2 * WINDOW, WINDOW), F32),
                        pltpu.VMEM((nblk, n_kv, ATT_GROUP // UNIT_HEADS, 1, UNIT_HEADS * WINDOW), F32),
                        pltpu.VMEM((2, UNIT_HEADS, 2 * WINDOW, WINDOW), BF16),
                        pltpu.VMEM((nblk, d, WINDOW), BF16)],
        compiler_params=pltpu.CompilerParams(
            dimension_semantics=("parallel", "parallel"), vmem_limit_bytes=VMEM_LIMIT_BYTES),
        name="swa_attention",
    )(x, g.reshape(1, d), rot, ind, bias, sink_rows, k4, k4, vt, vt, wq, w_o.astype(BF16))


def _ffn_layer(x, g, w_in, w_out):
    b, s, d = x.shape
    d_ff = w_out.shape[0]
    y = _ffn(x.reshape(b * s, d), g, w_in, w_out)
    return y.reshape(b, s, d)


def kernel(x, ffn1_norm, ffn1_w_in, ffn1_w_out, mix_norm, ffn2_norm, ffn2_w_in, ffn2_w_out,
           ret_w_in, ret_w_out, kv_norm, kv_w, k_norm, attn_w_q, q_norm, attn_sinks, attn_w_o):
    depth = ffn1_norm.shape[0]
    n_ret = ret_w_in.shape[0]
    k4 = v4 = None
    for i in range(depth):
        x = _ffn_layer(x, ffn1_norm[i], ffn1_w_in[i], ffn1_w_out[i])
        if i < n_ret:
            x = _retention_layer(x, mix_norm[i], ret_w_in[i], ret_w_out[i])
        else:
            j = i - n_ret
            x = _attention_layer(x, mix_norm[i], k4, v4, attn_w_q[j], q_norm[j], attn_sinks[j], attn_w_o[j])
        x = _ffn_layer(x, ffn2_norm[i], ffn2_w_in[i], ffn2_w_out[i])
        if i == n_ret - 1:
            k4, v4 = _shared_kv(x, kv_norm, kv_w, k_norm)
    return x
```

```python
import functools

import jax
import jax.numpy as jnp
import numpy as np
from jax import lax
from jax.experimental import pallas as pl
from jax.experimental.pallas import tpu as pltpu

F32 = jnp.float32
BF16 = jnp.bfloat16

NORM_EPS = 1e-6
MXU_TILE = 256
VMEM_LIMIT_BYTES = 56 * 1024 * 1024


def _resident(shape):
    return pl.BlockSpec(shape, lambda *_: (0,) * len(shape), pipeline_mode=pl.Buffered(1))


def _rms_scale(x):
    return lax.rsqrt(jnp.mean(x * x, axis=-1, keepdims=True) + NORM_EPS)


FFN_NORM_PIECES = 8


def _ffn_body(x_ref, xnext_ref, g_ref, win_ref, wo_ref, o_ref, h_ref, a_ref, *, ff_chunk):
    tm = x_ref.shape[0]
    d_ff = wo_ref.shape[0]
    n_chunks = d_ff // ff_chunk
    piece = tm // FFN_NORM_PIECES
    step = pl.program_id(0)
    cur, nxt = step % 2, (step + 1) % 2

    def norm_rows(src_ref, slot, rows):
        x = src_ref[rows, :]
        h_ref[slot, rows, :] = (x * _rms_scale(x) * g_ref[...]).astype(BF16)

    @pl.when(step == 0)
    def _():
        norm_rows(x_ref, 0, slice(0, tm))

    for c in range(n_chunks):
        sl = slice(c * ff_chunk, (c + 1) * ff_chunk)
        gate = jnp.dot(h_ref[cur], win_ref[:, sl], preferred_element_type=F32)
        up = jnp.dot(h_ref[cur], win_ref[:, d_ff + c * ff_chunk:d_ff + (c + 1) * ff_chunk],
                     preferred_element_type=F32)
        a_ref[:, sl] = (gate * jax.nn.sigmoid(gate) * up).astype(BF16)
        if c < FFN_NORM_PIECES:
            norm_rows(xnext_ref, nxt, slice(c * piece, (c + 1) * piece))
    y = jnp.dot(a_ref[...], wo_ref[...], preferred_element_type=F32)
    o_ref[...] = x_ref[...] + 0.5 * y


def _ffn(x2d, g, w_in, wo, *, tm=1024, ff_chunk=MXU_TILE):
    t, d = x2d.shape
    d_ff = wo.shape[0]
    steps = t // tm
    assert t % tm == 0 and d_ff % ff_chunk == 0 and d_ff // ff_chunk >= FFN_NORM_PIECES
    return pl.pallas_call(
        functools.partial(_ffn_body, ff_chunk=ff_chunk),
        grid=(steps,),
        in_specs=[
            pl.BlockSpec((tm, d), lambda i: (i, 0)),
            pl.BlockSpec((tm, d), lambda i: (jnp.minimum(i + 1, steps - 1), 0)),
            _resident((1, d)),
            _resident((d, 2 * d_ff)),
            _resident((d_ff, d)),
        ],
        out_specs=pl.BlockSpec((tm, d), lambda i: (i, 0)),
        out_shape=jax.ShapeDtypeStruct((t, d), F32),
        scratch_shapes=[pltpu.VMEM((2, tm, d), BF16), pltpu.VMEM((tm, d_ff), BF16)],
        compiler_params=pltpu.CompilerParams(
            dimension_semantics=("arbitrary",), vmem_limit_bytes=VMEM_LIMIT_BYTES),
        name="ffn",
    )(x2d, x2d, g.reshape(1, d), w_in.astype(BF16), wo.astype(BF16))


RET_DK = 256
RET_DV_FACTOR = 2
RET_ROT_BASE = 10000.0
NT_DIMS = (((1,), (1,)), ((), ()))
TN_DIMS = (((0,), (0,)), ((), ()))


def _ret_gamma(h):
    return 1.0 - 2.0 ** (-5.0 - h)


def _rotate_halves(t, cos, sin):
    half = t.shape[1] // 2
    te, to = t[:, :half], t[:, half:]
    return jnp.concatenate([te * cos - to * sin, to * cos + te * sin], axis=1)


def _ret_body(x_ref, g_ref, cq_ref, sq_ref, ck_ref, sk_ref, xi_ref, zeta_ref, dmask_ref, win_ref, wo_ref,
              o_ref, h_ref, gated_ref, state_ref, *, heads, dk, dv, chunk):
    rows = x_ref.shape[1]

    @pl.when(pl.program_id(1) == 0)
    def _():
        state_ref[...] = jnp.zeros_like(state_ref)

    x = x_ref[0]
    h_ref[...] = (x * _rms_scale(x) * g_ref[...]).astype(BF16)
    k_off, v_off, g_off = heads * dk, 2 * heads * dk, 2 * heads * dk + heads * dv
    for hd in range(heads):
        proj = lambda off, width: jnp.dot(
            h_ref[...], win_ref[:, off + hd * width: off + (hd + 1) * width], preferred_element_type=F32)
        qr = _rotate_halves(proj(0, dk), cq_ref[...], sq_ref[...])
        kr = _rotate_halves(proj(k_off, dk), ck_ref[...], sk_ref[...])
        v = proj(v_off, dv).astype(BF16)
        gate = proj(g_off, dv)
        gate = gate * jax.nn.sigmoid(gate)
        state = state_ref[hd]
        for c in range(rows // chunk):
            sl = slice(c * chunk, (c + 1) * chunk)
            qc, kc, vc = qr[sl], kr[sl], v[sl]
            scores = lax.dot_general(qc.astype(BF16), kc.astype(BF16), NT_DIMS, preferred_element_type=F32)
            scores = (scores * dmask_ref[hd]).astype(BF16)
            out = (jnp.dot(scores, vc, preferred_element_type=F32)
                   + jnp.dot((qc * xi_ref[hd]).astype(BF16), state.astype(BF16), preferred_element_type=F32))
            kz = (kc * zeta_ref[hd]).astype(BF16)
            state = state * (_ret_gamma(hd) ** chunk) + lax.dot_general(kz, vc, TN_DIMS, preferred_element_type=F32)
            out = out * _rms_scale(out)
            gated_ref[sl, hd * dv:(hd + 1) * dv] = (out * gate[sl]).astype(BF16)
        state_ref[hd] = state
    y = jnp.dot(gated_ref[...], wo_ref[...], preferred_element_type=F32)
    o_ref[0] = x_ref[0] + y


def _ret_tables(s, chunk, heads, dk):
    half = dk // 2
    freq = 1.0 / (RET_ROT_BASE ** np.linspace(0.0, 1.0, half))
    ang = np.arange(s)[:, None] * freq[None, :]
    cos, sin = np.cos(ang), np.sin(ang)
    k_scale = dk ** -0.5
    gamma = np.array([_ret_gamma(h) for h in range(heads)])
    n = np.arange(chunk)
    diff = n[:, None] - n[None, :]
    dmask = np.where(diff[None] >= 0, gamma[:, None, None] ** np.maximum(diff, 0)[None], 0.0)
    xi = gamma[:, None] ** (n[None, :] + 1.0)
    zeta = gamma[:, None] ** (chunk - 1.0 - n[None, :])
    widen = lambda t: np.broadcast_to(t[:, :, None], (heads, chunk, dk))
    tables = (cos, sin, cos * k_scale, sin * k_scale, widen(xi), widen(zeta), dmask)
    return tuple(jnp.asarray(t, F32) for t in tables)


def _deinterleave_heads(w, heads, dk):
    d = w.shape[0]
    return w.reshape(d, heads, dk // 2, 2).transpose(0, 1, 3, 2).reshape(d, heads * dk)


def _retention_layer(x, g, w_in, w_out, *, chunk=MXU_TILE, ts=2 * MXU_TILE):
    b, s, d = x.shape
    heads = d // RET_DK
    dk, dv = RET_DK, RET_DV_FACTOR * d // heads
    assert s % ts == 0 and ts % chunk == 0
    dq = heads * dk
    win = jnp.concatenate([_deinterleave_heads(w_in[:, :dq], heads, dk),
                           _deinterleave_heads(w_in[:, dq:2 * dq], heads, dk),
                           w_in[:, 2 * dq:]], axis=1).astype(BF16)
    cq, sq, ck, sk, xi, zeta, dmask = _ret_tables(s, chunk, heads, dk)
    rot_spec = pl.BlockSpec((ts, dk // 2), lambda i, j: (j, 0))
    return pl.pallas_call(
        functools.partial(_ret_body, heads=heads, dk=dk, dv=dv, chunk=chunk),
        grid=(b, s // ts),
        in_specs=[
            pl.BlockSpec((1, ts, d), lambda i, j: (i, j, 0)),
            _resident((1, d)),
            rot_spec, rot_spec, rot_spec, rot_spec,
            _resident((heads, chunk, dk)),
            _resident((heads, chunk, dk)),
            _resident((heads, chunk, chunk)),
            _resident(win.shape),
            _resident(w_out.shape),
        ],
        out_specs=pl.BlockSpec((1, ts, d), lambda i, j: (i, j, 0)),
        out_shape=jax.ShapeDtypeStruct((b, s, d), F32),
        scratch_shapes=[pltpu.VMEM((ts, d), BF16), pltpu.VMEM((ts, heads * dv), BF16),
                        pltpu.VMEM((heads, dk, dv), F32)],
        compiler_params=pltpu.CompilerParams(
            dimension_semantics=("parallel", "arbitrary"), vmem_limit_bytes=VMEM_LIMIT_BYTES),
        name="retention",
    )(x, g.reshape(1, d), cq, sq, ck, sk, xi, zeta, dmask, win, w_out.astype(BF16))


ATT_HD = 64
ATT_GROUP = 8
SLAB_HEADS = MXU_TILE // ATT_HD
WINDOW = 128
ROPE_THETA = 10000.0
NEG_INF = -1e30
LOG2_E = 1.4426950408889634
UNIT_HEADS = 8
ONES_ROWS = 16


def _rope_tables(s, scale):
    half = ATT_HD // 2
    inv_freq = 1.0 / (ROPE_THETA ** (np.arange(0, ATT_HD, 2) / ATT_HD))
    ang = np.arange(s)[:, None] * inv_freq[None, :]
    return (jnp.asarray(np.tile(np.cos(ang), (1, SLAB_HEADS)) * scale, F32),
            jnp.asarray(np.tile(np.sin(ang), (1, SLAB_HEADS)) * scale, F32))


def _slab_gain(g):
    half = ATT_HD // 2
    return jnp.concatenate([jnp.tile(g[:half], SLAB_HEADS), jnp.tile(g[half:], SLAB_HEADS)]).reshape(1, MXU_TILE)


def _kv_body(x_ref, g_ref, kg_ref, cos_ref, sin_ref, wk_ref, wv_ref, k_ref, vt_ref, *, kv_heads):
    x = x_ref[0]
    h = (x * _rms_scale(x) * g_ref[...]).astype(BF16)
    for g in range(kv_heads):
        sl = slice(g * MXU_TILE, (g + 1) * MXU_TILE)
        k = jnp.dot(h, wk_ref[:, sl], preferred_element_type=F32)
        k = k * _rms_scale(k) * kg_ref[...]
        k_ref[0, :, sl] = _rotate_halves(k, cos_ref[...], sin_ref[...]).astype(BF16)
    v = jnp.dot(h, wv_ref[...], preferred_element_type=F32)
    vt_ref[0] = v.T.astype(BF16)


def _shared_kv(x, kv_norm, kv_w, k_norm, *, ts=1024):
    b, s, d = x.shape
    kv_heads = kv_w.shape[1] // (2 * ATT_HD)
    half = ATT_HD // 2
    wk = kv_w[:, :kv_heads * ATT_HD].reshape(d, kv_heads, ATT_HD)
    wk = jnp.concatenate([jnp.tile(wk[:, :, :half], (1, 1, SLAB_HEADS)),
                          jnp.tile(wk[:, :, half:], (1, 1, SLAB_HEADS))], axis=2)
    wk = wk.reshape(d, kv_heads * MXU_TILE).astype(BF16)
    wv = kv_w[:, kv_heads * ATT_HD:].astype(BF16)
    cos, sin = _rope_tables(s, 1.0)
    rot_spec = pl.BlockSpec((ts, MXU_TILE // 2), lambda i, j: (j, 0))
    return pl.pallas_call(
        functools.partial(_kv_body, kv_heads=kv_heads),
        grid=(b, s // ts),
        in_specs=[pl.BlockSpec((1, ts, d), lambda i, j: (i, j, 0)), _resident((1, d)), _resident((1, MXU_TILE)),
                  rot_spec, rot_spec, _resident(wk.shape), _resident(wv.shape)],
        out_specs=[pl.BlockSpec((1, ts, kv_heads * MXU_TILE), lambda i, j: (i, j, 0)),
                   pl.BlockSpec((1, kv_heads * ATT_HD, ts), lambda i, j: (i, 0, j))],
        out_shape=[jax.ShapeDtypeStruct((b, s, kv_heads * MXU_TILE), BF16),
                   jax.ShapeDtypeStruct((b, kv_heads * ATT_HD, s), BF16)],
        compiler_params=pltpu.CompilerParams(
            dimension_semantics=("parallel", "parallel"), vmem_limit_bytes=VMEM_LIMIT_BYTES),
        name="shared_kv",
    )(x, kv_norm.reshape(1, d), _slab_gain(k_norm), cos, sin, wk, wv)


def _attn_body(x_ref, g_ref, rot_ref, ind_ref, bias_ref, sink_ref, kprev_ref, kcur_ref,
               vprev_ref, vcur_ref, wq_ref, wo_ref, o_ref, h_ref, lhs_ref, st_ref, att_ref):
    tq, d = x_ref.shape[1], x_ref.shape[2]
    n_slabs = d // MXU_TILE
    slabs_per_kv = ATT_GROUP // SLAB_HEADS
    n_kv = n_slabs // slabs_per_kv
    n_blk = tq // WINDOW
    x = x_ref[0]
    h_ref[...] = (x * _rms_scale(x) * g_ref[...]).astype(BF16)
    lane = lax.broadcasted_iota(jnp.int32, (WINDOW, MXU_TILE), 1)
    q_lanes = [(lane % (MXU_TILE // 2)) // (ATT_HD // 2) == j for j in range(SLAB_HEADS)]
    first = jnp.where(pl.program_id(1) == 0, 1, 0)

    def project(slab):
        return jnp.dot(h_ref[...], wq_ref[:, slab * MXU_TILE:(slab + 1) * MXU_TILE], preferred_element_type=F32)

    half = MXU_TILE // 2
    q_next = project(0)
    for slab in range(n_slabs):
        q = q_next
        if slab + 1 < n_slabs:
            q_next = project(slab + 1)
        ms = jnp.dot((q * q).astype(BF16), ind_ref[...], preferred_element_type=F32)
        q = q * lax.rsqrt(ms + NORM_EPS)
        q1, q2 = q[:, :half], q[:, half:]
        q = jnp.concatenate([q1 * rot_ref[0] - q2 * rot_ref[1], q2 * rot_ref[2] + q1 * rot_ref[3]], axis=1)
        q = q.astype(BF16)
        kv, sub = divmod(slab, slabs_per_kv)
        for blk in range(n_blk):
            rows = q[blk * WINDOW:(blk + 1) * WINDOW]
            for j in range(SLAB_HEADS):
                r0 = (sub * SLAB_HEADS + j) * WINDOW
                lhs_ref[blk, kv, r0:r0 + WINDOW, :] = jnp.where(q_lanes[j], rows, jnp.zeros_like(rows))

    cols = UNIT_HEADS * WINDOW
    parts = ATT_GROUP // UNIT_HEADS
    for blk in range(n_blk):
        for kv in range(n_kv):
            k_sl = slice(kv * MXU_TILE, (kv + 1) * MXU_TILE)
            if blk == 0:
                keys = jnp.concatenate([kprev_ref[0, :, k_sl], kcur_ref[0, :WINDOW, k_sl]], axis=0)
            else:
                keys = kcur_ref[0, (blk - 1) * WINDOW:(blk + 1) * WINDOW, k_sl]
            for part in range(parts):
                c_sl = slice(part * cols, (part + 1) * cols)
                st_ref[blk, kv, part] = lax.dot_general(
                    keys, lhs_ref[blk, kv, c_sl, :], NT_DIMS, preferred_element_type=F32)
    for blk in range(n_blk):
        for kv in range(n_kv):
            v_sl = slice(kv * ATT_HD, (kv + 1) * ATT_HD)
            if blk == 0:
                vals_t = jnp.concatenate([vprev_ref[0, v_sl, :], vcur_ref[0, v_sl, :WINDOW]], axis=1)
                bias = bias_ref[first]
            else:
                vals_t = vcur_ref[0, v_sl, (blk - 1) * WINDOW:(blk + 1) * WINDOW]
                bias = bias_ref[0]
            vals_t = jnp.concatenate([vals_t, jnp.ones((ONES_ROWS, 2 * WINDOW), BF16)], axis=0)
            for part in range(parts):
                c_sl = slice(part * cols, (part + 1) * cols)
                st = st_ref[blk, kv, part] + bias
                sink = sink_ref[kv, :, c_sl]
                m = jnp.maximum(jnp.max(st, axis=0, keepdims=True), sink)
                p = jnp.exp2(st - m).astype(BF16)
                ot = jnp.dot(vals_t, p, preferred_element_type=F32)
                denom = ot[ATT_HD:ATT_HD + 1, :] + jnp.exp2(sink - m)
                ot = ot[:ATT_HD, :] * (1.0 / denom)
                for j in range(UNIT_HEADS):
                    head = kv * ATT_GROUP + part * UNIT_HEADS + j
                    att_ref[head * ATT_HD:(head + 1) * ATT_HD, blk * WINDOW:(blk + 1) * WINDOW] = (
                        ot[:, j * WINDOW:(j + 1) * WINDOW].astype(BF16))
    y = lax.dot_general(att_ref[...], wo_ref[...], TN_DIMS, preferred_element_type=F32)
    o_ref[0] = x_ref[0] + y


def _attention_layer(x, g, k4, vt, w_q, q_norm, sinks, w_o, *, tq=512):
    b, s, d = x.shape
    heads = d // ATT_HD
    n_slabs = heads // SLAB_HEADS
    half = ATT_HD // 2
    nblk = tq // WINDOW
    assert s % tq == 0
    wq = w_q.reshape(d, n_slabs, SLAB_HEADS, 2, half).transpose(0, 1, 3, 2, 4).reshape(d, d).astype(BF16)
    cos, sin = _rope_tables(s, ATT_HD ** -0.5 * LOG2_E)
    g1, g2 = jnp.tile(q_norm[:half], SLAB_HEADS)[None, :], jnp.tile(q_norm[half:], SLAB_HEADS)[None, :]
    rot = jnp.stack([cos * g1, sin * g2, cos * g2, sin * g1])
    lane = np.arange(MXU_TILE)
    head_of = (lane % (MXU_TILE // 2)) // half
    ind = jnp.asarray((head_of[:, None] == head_of[None, :]) / ATT_HD, BF16)
    kj = np.arange(2 * WINDOW)[:, None]
    qi = np.tile(np.arange(WINDOW), UNIT_HEADS)[None, :]
    band = (kj > qi) & (kj <= qi + WINDOW)
    bias = jnp.asarray(np.stack([np.where(band, 0.0, NEG_INF), np.where(band & (kj >= WINDOW), 0.0, NEG_INF)]), F32)
    n_kv = heads // ATT_GROUP
    sink_rows = jnp.repeat(sinks.astype(F32).reshape(n_kv, ATT_GROUP) * LOG2_E, WINDOW, axis=1)[:, None, :]
    prev_blk = lambda j: jnp.maximum(j * nblk - 1, 0)
    return pl.pallas_call(
        _attn_body,
        grid=(b, s // tq),
        in_specs=[
            pl.BlockSpec((1, tq, d), lambda i, j: (i, j, 0)),
            _resident((1, d)),
            pl.BlockSpec((4, tq, MXU_TILE // 2), lambda i, j: (0, j, 0)),
            _resident(ind.shape), _resident(bias.shape), _resident(sink_rows.shape),
            pl.BlockSpec((1, WINDOW, k4.shape[2]), lambda i, j: (i, prev_blk(j), 0)),
            pl.BlockSpec((1, tq, k4.shape[2]), lambda i, j: (i, j, 0)),
            pl.BlockSpec((1, vt.shape[1], WINDOW), lambda i, j: (i, 0, prev_blk(j))),
            pl.BlockSpec((1, vt.shape[1], tq), lambda i, j: (i, 0, j)),
            _resident(wq.shape), _resident(w_o.shape),
        ],
        out_specs=pl.BlockSpec((1, tq, d), lambda i, j: (i, j, 0)),
        out_shape=jax.ShapeDtypeStruct((b, s, d), F32),
        scratch_shapes=[pltpu.VMEM((tq, d), BF16),
                        pltpu.VMEM((nblk, n_kv, ATT_GROUP * WINDOW, MXU_TILE), BF16),
                        pltpu.VMEM((nblk, n_kv, ATT_GROUP // UNIT_HEADS, 2 * WINDOW, UNIT_HEADS * WINDOW), F32),
                        pltpu.VMEM((d, tq), BF16)],
        compiler_params=pltpu.CompilerParams(
            dimension_semantics=("parallel", "parallel"), vmem_limit_bytes=VMEM_LIMIT_BYTES),
        name="swa_attention",
    )(x, g.reshape(1, d), rot, ind, bias, sink_rows, k4, k4, vt, vt, wq, w_o.astype(BF16))


def _ffn_layer(x, g, w_in, w_out):
    b, s, d = x.shape
    return _ffn(x.reshape(b * s, d), g, w_in, w_out).reshape(b, s, d)


def kernel(x, ffn1_norm, ffn1_w_in, ffn1_w_out, mix_norm, ffn2_norm, ffn2_w_in, ffn2_w_out,
           ret_w_in, ret_w_out, kv_norm, kv_w, k_norm, attn_w_q, q_norm, attn_sinks, attn_w_o):
    depth = ffn1_norm.shape[0]
    n_ret = ret_w_in.shape[0]
    k4 = v4 = None
    for i in range(depth):
        x = _ffn_layer(x, ffn1_norm[i], ffn1_w_in[i], ffn1_w_out[i])
        if i < n_ret:
            x = _retention_layer(x, mix_norm[i], ret_w_in[i], ret_w_out[i])
        else:
            j = i - n_ret
            x = _attention_layer(x, mix_norm[i], k4, v4, attn_w_q[j], q_norm[j], attn_sinks[j], attn_w_o[j])
        x = _ffn_layer(x, ffn2_norm[i], ffn2_w_in[i], ffn2_w_out[i])
        if i == n_ret - 1:
            k4, v4 = _shared_kv(x, kv_norm, kv_w, k_norm)
    return x
```

```python
import functools

import jax
import jax.numpy as jnp
import numpy as np
from jax import lax
from jax.experimental import pallas as pl
from jax.experimental.pallas import tpu as pltpu

F32 = jnp.float32
BF16 = jnp.bfloat16

NORM_EPS = 1e-6
MXU_TILE = 256
VMEM_LIMIT_BYTES = 56 * 1024 * 1024


def _resident(shape):
    return pl.BlockSpec(shape, lambda *_: (0,) * len(shape), pipeline_mode=pl.Buffered(1))


def _rms_scale(x):
    return lax.rsqrt(jnp.mean(x * x, axis=-1, keepdims=True) + NORM_EPS)


FFN_NORM_PIECES = 8


def _ffn_body(x_ref, xnext_ref, g_ref, win_ref, wo_ref, o_ref, h_ref, a_ref, *, ff_chunk):
    tm = x_ref.shape[0]
    d_ff = wo_ref.shape[0]
    n_chunks = d_ff // ff_chunk
    piece = tm // FFN_NORM_PIECES
    step = pl.program_id(0)
    cur, nxt = step % 2, (step + 1) % 2

    def norm_rows(src_ref, slot, rows):
        x = src_ref[rows, :]
        h_ref[slot, rows, :] = (x * _rms_scale(x) * g_ref[...]).astype(BF16)

    @pl.when(step == 0)
    def _():
        norm_rows(x_ref, 0, slice(0, tm))

    for c in range(n_chunks):
        sl = slice(c * ff_chunk, (c + 1) * ff_chunk)
        gate = jnp.dot(h_ref[cur], win_ref[:, sl], preferred_element_type=F32)
        up = jnp.dot(h_ref[cur], win_ref[:, d_ff + c * ff_chunk:d_ff + (c + 1) * ff_chunk],
                     preferred_element_type=F32)
        a_ref[:, sl] = (gate * jax.nn.sigmoid(gate) * up).astype(BF16)
        if c < FFN_NORM_PIECES:
            norm_rows(xnext_ref, nxt, slice(c * piece, (c + 1) * piece))
    y = jnp.dot(a_ref[...], wo_ref[...], preferred_element_type=F32)
    o_ref[...] = x_ref[...] + 0.5 * y


def _layer_resident(stacked, layer):
    return pl.BlockSpec((None,) + stacked.shape[1:], lambda *_: (layer, 0, 0), pipeline_mode=pl.Buffered(1))


def _ffn(x2d, g, w_in, wo, layer, *, tm=1024, ff_chunk=MXU_TILE):
    t, d = x2d.shape
    d_ff = wo.shape[1]
    steps = t // tm
    assert t % tm == 0 and d_ff % ff_chunk == 0 and d_ff // ff_chunk >= FFN_NORM_PIECES
    return pl.pallas_call(
        functools.partial(_ffn_body, ff_chunk=ff_chunk),
        grid=(steps,),
        in_specs=[
            pl.BlockSpec((tm, d), lambda i: (i, 0)),
            pl.BlockSpec((tm, d), lambda i: (jnp.minimum(i + 1, steps - 1), 0)),
            _resident((1, d)),
            _layer_resident(w_in, layer),
            _layer_resident(wo, layer),
        ],
        out_specs=pl.BlockSpec((tm, d), lambda i: (i, 0)),
        out_shape=jax.ShapeDtypeStruct((t, d), F32),
        scratch_shapes=[pltpu.VMEM((2, tm, d), BF16), pltpu.VMEM((tm, d_ff), BF16)],
        compiler_params=pltpu.CompilerParams(
            dimension_semantics=("arbitrary",), vmem_limit_bytes=VMEM_LIMIT_BYTES),
        name="ffn",
    )(x2d, x2d, g.reshape(1, d), w_in, wo)


RET_DK = 256
RET_DV_FACTOR = 2
RET_ROT_BASE = 10000.0
NT_DIMS = (((1,), (1,)), ((), ()))
TN_DIMS = (((0,), (0,)), ((), ()))


def _ret_gamma(h):
    return 1.0 - 2.0 ** (-5.0 - h)


def _rotate_halves(t, cos, sin):
    half = t.shape[1] // 2
    te, to = t[:, :half], t[:, half:]
    return jnp.concatenate([te * cos - to * sin, to * cos + te * sin], axis=1)


def _ret_body(x_ref, g_ref, cq_ref, sq_ref, ck_ref, sk_ref, xi_ref, zeta_ref, dmask_ref, wqk_ref, wvg_ref, wo_ref,
              o_ref, h_ref, gated_ref, state_ref, *, heads, dk, dv, chunk):
    rows = x_ref.shape[1]

    @pl.when(pl.program_id(1) == 0)
    def _():
        state_ref[...] = jnp.zeros_like(state_ref)

    x = x_ref[0]
    h_ref[...] = (x * _rms_scale(x) * g_ref[...]).astype(BF16)
    for hd in range(heads):
        proj = lambda w_ref, off, width: jnp.dot(
            h_ref[...], w_ref[:, off + hd * width: off + (hd + 1) * width], preferred_element_type=F32)
        qr = _rotate_halves(proj(wqk_ref, 0, dk), cq_ref[...], sq_ref[...])
        kr = _rotate_halves(proj(wqk_ref, heads * dk, dk), ck_ref[...], sk_ref[...])
        v = proj(wvg_ref, 0, dv).astype(BF16)
        gate = proj(wvg_ref, heads * dv, dv)
        gate = gate * jax.nn.sigmoid(gate)
        state = state_ref[hd]
        for c in range(rows // chunk):
            sl = slice(c * chunk, (c + 1) * chunk)
            qc, kc, vc = qr[sl], kr[sl], v[sl]
            scores = lax.dot_general(qc.astype(BF16), kc.astype(BF16), NT_DIMS, preferred_element_type=F32)
            scores = (scores * dmask_ref[hd]).astype(BF16)
            out = (jnp.dot(scores, vc, preferred_element_type=F32)
                   + jnp.dot((qc * xi_ref[hd]).astype(BF16), state.astype(BF16), preferred_element_type=F32))
            kz = (kc * zeta_ref[hd]).astype(BF16)
            state = state * (_ret_gamma(hd) ** chunk) + lax.dot_general(kz, vc, TN_DIMS, preferred_element_type=F32)
            out = out * _rms_scale(out)
            gated_ref[sl, hd * dv:(hd + 1) * dv] = (out * gate[sl]).astype(BF16)
        state_ref[hd] = state
    y = jnp.dot(gated_ref[...], wo_ref[...], preferred_element_type=F32)
    o_ref[0] = x_ref[0] + y


def _ret_tables(s, chunk, heads, dk):
    half = dk // 2
    freq = 1.0 / (RET_ROT_BASE ** np.linspace(0.0, 1.0, half))
    ang = np.arange(s)[:, None] * freq[None, :]
    cos, sin = np.cos(ang), np.sin(ang)
    k_scale = dk ** -0.5
    gamma = np.array([_ret_gamma(h) for h in range(heads)])
    n = np.arange(chunk)
    diff = n[:, None] - n[None, :]
    dmask = np.where(diff[None] >= 0, gamma[:, None, None] ** np.maximum(diff, 0)[None], 0.0)
    xi = gamma[:, None] ** (n[None, :] + 1.0)
    zeta = gamma[:, None] ** (chunk - 1.0 - n[None, :])
    widen = lambda t: np.broadcast_to(t[:, :, None], (heads, chunk, dk))
    tables = (cos, sin, cos * k_scale, sin * k_scale, widen(xi), widen(zeta), dmask)
    return tuple(jnp.asarray(t, F32) for t in tables)


def _deinterleave_heads(w, heads, dk):
    d = w.shape[0]
    return w.reshape(d, heads, dk // 2, 2).transpose(0, 1, 3, 2).reshape(d, heads * dk)


def _retention_layer(x, g, w_in, w_out, *, chunk=MXU_TILE, ts=2 * MXU_TILE):
    b, s, d = x.shape
    heads = d // RET_DK
    dk, dv = RET_DK, RET_DV_FACTOR * d // heads
    assert s % ts == 0 and ts % chunk == 0
    dq = heads * dk
    wqk = _deinterleave_heads(w_in[:, :2 * dq], 2 * heads, dk).astype(BF16)
    wvg = w_in[:, 2 * dq:].astype(BF16)
    cq, sq, ck, sk, xi, zeta, dmask = _ret_tables(s, chunk, heads, dk)
    rot_spec = pl.BlockSpec((ts, dk // 2), lambda i, j: (j, 0))
    return pl.pallas_call(
        functools.partial(_ret_body, heads=heads, dk=dk, dv=dv, chunk=chunk),
        grid=(b, s // ts),
        in_specs=[
            pl.BlockSpec((1, ts, d), lambda i, j: (i, j, 0)),
            _resident((1, d)),
            rot_spec, rot_spec, rot_spec, rot_spec,
            _resident((heads, chunk, dk)),
            _resident((heads, chunk, dk)),
            _resident((heads, chunk, chunk)),
            _resident(wqk.shape),
            _resident(wvg.shape),
            _resident(w_out.shape),
        ],
        out_specs=pl.BlockSpec((1, ts, d), lambda i, j: (i, j, 0)),
        out_shape=jax.ShapeDtypeStruct((b, s, d), F32),
        scratch_shapes=[pltpu.VMEM((ts, d), BF16), pltpu.VMEM((ts, heads * dv), BF16),
                        pltpu.VMEM((heads, dk, dv), F32)],
        compiler_params=pltpu.CompilerParams(
            dimension_semantics=("parallel", "arbitrary"), vmem_limit_bytes=VMEM_LIMIT_BYTES),
        name="retention",
    )(x, g.reshape(1, d), cq, sq, ck, sk, xi, zeta, dmask, wqk, wvg, w_out.astype(BF16))


ATT_HD = 64
ATT_GROUP = 8
SLAB_HEADS = MXU_TILE // ATT_HD
WINDOW = 128
ROPE_THETA = 10000.0
NEG_INF = -1e30
LOG2_E = 1.4426950408889634
UNIT_HEADS = 8
ONES_ROWS = 16


def _rope_tables(s, scale):
    half = ATT_HD // 2
    inv_freq = 1.0 / (ROPE_THETA ** (np.arange(0, ATT_HD, 2) / ATT_HD))
    ang = np.arange(s)[:, None] * inv_freq[None, :]
    return (jnp.asarray(np.tile(np.cos(ang), (1, SLAB_HEADS)) * scale, F32),
            jnp.asarray(np.tile(np.sin(ang), (1, SLAB_HEADS)) * scale, F32))


def _slab_gain(g):
    half = ATT_HD // 2
    return jnp.concatenate([jnp.tile(g[:half], SLAB_HEADS), jnp.tile(g[half:], SLAB_HEADS)]).reshape(1, MXU_TILE)


def _kv_body(x_ref, g_ref, kg_ref, cos_ref, sin_ref, wk_ref, wv_ref, k_ref, vt_ref, *, kv_heads):
    x = x_ref[0]
    h = (x * _rms_scale(x) * g_ref[...]).astype(BF16)
    for g in range(kv_heads):
        sl = slice(g * MXU_TILE, (g + 1) * MXU_TILE)
        k = jnp.dot(h, wk_ref[:, sl], preferred_element_type=F32)
        k = k * _rms_scale(k) * kg_ref[...]
        k_ref[0, :, sl] = _rotate_halves(k, cos_ref[...], sin_ref[...]).astype(BF16)
    v = jnp.dot(h, wv_ref[...], preferred_element_type=F32)
    vt_ref[0] = v.T.astype(BF16)


def _shared_kv(x, kv_norm, kv_w, k_norm, *, ts=1024):
    b, s, d = x.shape
    kv_heads = kv_w.shape[1] // (2 * ATT_HD)
    half = ATT_HD // 2
    wk = kv_w[:, :kv_heads * ATT_HD].reshape(d, kv_heads, ATT_HD)
    wk = jnp.concatenate([jnp.tile(wk[:, :, :half], (1, 1, SLAB_HEADS)),
                          jnp.tile(wk[:, :, half:], (1, 1, SLAB_HEADS))], axis=2)
    wk = wk.reshape(d, kv_heads * MXU_TILE).astype(BF16)
    wv = kv_w[:, kv_heads * ATT_HD:].astype(BF16)
    cos, sin = _rope_tables(s, 1.0)
    rot_spec = pl.BlockSpec((ts, MXU_TILE // 2), lambda i, j: (j, 0))
    return pl.pallas_call(
        functools.partial(_kv_body, kv_heads=kv_heads),
        grid=(b, s // ts),
        in_specs=[pl.BlockSpec((1, ts, d), lambda i, j: (i, j, 0)), _resident((1, d)), _resident((1, MXU_TILE)),
                  rot_spec, rot_spec, _resident(wk.shape), _resident(wv.shape)],
        out_specs=[pl.BlockSpec((1, ts, kv_heads * MXU_TILE), lambda i, j: (i, j, 0)),
                   pl.BlockSpec((1, kv_heads * ATT_HD, ts), lambda i, j: (i, 0, j))],
        out_shape=[jax.ShapeDtypeStruct((b, s, kv_heads * MXU_TILE), BF16),
                   jax.ShapeDtypeStruct((b, kv_heads * ATT_HD, s), BF16)],
        compiler_params=pltpu.CompilerParams(
            dimension_semantics=("parallel", "parallel"), vmem_limit_bytes=VMEM_LIMIT_BYTES),
        name="shared_kv",
    )(x, kv_norm.reshape(1, d), _slab_gain(k_norm), cos, sin, wk, wv)


def _attn_body(x_ref, g_ref, rot_ref, ind_ref, bias_ref, sink_ref, kprev_ref, kcur_ref,
               vprev_ref, vcur_ref, wq_ref, wo_ref, o_ref, h_ref, lhs_ref, st_ref, att_ref):
    tq, d = x_ref.shape[1], x_ref.shape[2]
    n_slabs = d // MXU_TILE
    slabs_per_kv = ATT_GROUP // SLAB_HEADS
    n_kv = n_slabs // slabs_per_kv
    n_blk = tq // WINDOW
    x = x_ref[0]
    h_ref[...] = (x * _rms_scale(x) * g_ref[...]).astype(BF16)
    lane = lax.broadcasted_iota(jnp.int32, (WINDOW, MXU_TILE), 1)
    q_lanes = [(lane % (MXU_TILE // 2)) // (ATT_HD // 2) == j for j in range(SLAB_HEADS)]
    first = jnp.where(pl.program_id(1) == 0, 1, 0)

    def project(slab):
        return jnp.dot(h_ref[...], wq_ref[:, slab * MXU_TILE:(slab + 1) * MXU_TILE], preferred_element_type=F32)

    half = MXU_TILE // 2
    q_next = project(0)
    for slab in range(n_slabs):
        q = q_next
        if slab + 1 < n_slabs:
            q_next = project(slab + 1)
        ms = jnp.dot((q * q).astype(BF16), ind_ref[...], preferred_element_type=F32)
        q = q * lax.rsqrt(ms + NORM_EPS)
        q1, q2 = q[:, :half], q[:, half:]
        q = jnp.concatenate([q1 * rot_ref[0] - q2 * rot_ref[1], q2 * rot_ref[2] + q1 * rot_ref[3]], axis=1)
        q = q.astype(BF16)
        kv, sub = divmod(slab, slabs_per_kv)
        for blk in range(n_blk):
            rows = q[blk * WINDOW:(blk + 1) * WINDOW]
            for j in range(SLAB_HEADS):
                r0 = (sub * SLAB_HEADS + j) * WINDOW
                lhs_ref[blk, kv, r0:r0 + WINDOW, :] = jnp.where(q_lanes[j], rows, jnp.zeros_like(rows))

    cols = UNIT_HEADS * WINDOW
    parts = ATT_GROUP // UNIT_HEADS
    for blk in range(n_blk):
        for kv in range(n_kv):
            k_sl = slice(kv * MXU_TILE, (kv + 1) * MXU_TILE)
            if blk == 0:
                keys = jnp.concatenate([kprev_ref[0, :, k_sl], kcur_ref[0, :WINDOW, k_sl]], axis=0)
            else:
                keys = kcur_ref[0, (blk - 1) * WINDOW:(blk + 1) * WINDOW, k_sl]
            for part in range(parts):
                c_sl = slice(part * cols, (part + 1) * cols)
                st_ref[blk, kv, part] = lax.dot_general(
                    keys, lhs_ref[blk, kv, c_sl, :], NT_DIMS, preferred_element_type=F32)
    for blk in range(n_blk):
        for kv in range(n_kv):
            v_sl = slice(kv * ATT_HD, (kv + 1) * ATT_HD)
            if blk == 0:
                vals_t = jnp.concatenate([vprev_ref[0, v_sl, :], vcur_ref[0, v_sl, :WINDOW]], axis=1)
                bias = bias_ref[first]
            else:
                vals_t = vcur_ref[0, v_sl, (blk - 1) * WINDOW:(blk + 1) * WINDOW]
                bias = bias_ref[0]
            vals_t = jnp.concatenate([vals_t, jnp.ones((ONES_ROWS, 2 * WINDOW), BF16)], axis=0)
            for part in range(parts):
                c_sl = slice(part * cols, (part + 1) * cols)
                st = st_ref[blk, kv, part] + bias
                sink = sink_ref[kv, :, c_sl]
                m = jnp.maximum(jnp.max(st, axis=0, keepdims=True), sink)
                p = jnp.exp2(st - m).astype(BF16)
                ot = jnp.dot(vals_t, p, preferred_element_type=F32)
                denom = ot[ATT_HD:ATT_HD + 1, :] + jnp.exp2(sink - m)
                ot = ot[:ATT_HD, :] * (1.0 / denom)
                for j in range(UNIT_HEADS):
                    head = kv * ATT_GROUP + part * UNIT_HEADS + j
                    att_ref[head * ATT_HD:(head + 1) * ATT_HD, blk * WINDOW:(blk + 1) * WINDOW] = (
                        ot[:, j * WINDOW:(j + 1) * WINDOW].astype(BF16))
    y = lax.dot_general(att_ref[...], wo_ref[...], TN_DIMS, preferred_element_type=F32)
    o_ref[0] = x_ref[0] + y


def _attention_layer(x, g, k4, vt, w_q, q_norm, sinks, w_o, *, tq=512):
    b, s, d = x.shape
    heads = d // ATT_HD
    n_slabs = heads // SLAB_HEADS
    half = ATT_HD // 2
    nblk = tq // WINDOW
    assert s % tq == 0
    wq = w_q.reshape(d, n_slabs, SLAB_HEADS, 2, half).transpose(0, 1, 3, 2, 4).reshape(d, d).astype(BF16)
    cos, sin = _rope_tables(s, ATT_HD ** -0.5 * LOG2_E)
    g1, g2 = jnp.tile(q_norm[:half], SLAB_HEADS)[None, :], jnp.tile(q_norm[half:], SLAB_HEADS)[None, :]
    rot = jnp.stack([cos * g1, sin * g2, cos * g2, sin * g1])
    lane = np.arange(MXU_TILE)
    head_of = (lane % (MXU_TILE // 2)) // half
    ind = jnp.asarray((head_of[:, None] == head_of[None, :]) / ATT_HD, BF16)
    kj = np.arange(2 * WINDOW)[:, None]
    qi = np.tile(np.arange(WINDOW), UNIT_HEADS)[None, :]
    band = (kj > qi) & (kj <= qi + WINDOW)
    bias = jnp.asarray(np.stack([np.where(band, 0.0, NEG_INF), np.where(band & (kj >= WINDOW), 0.0, NEG_INF)]), F32)
    n_kv = heads // ATT_GROUP
    sink_rows = jnp.repeat(sinks.astype(F32).reshape(n_kv, ATT_GROUP) * LOG2_E, WINDOW, axis=1)[:, None, :]
    prev_blk = lambda j: jnp.maximum(j * nblk - 1, 0)
    return pl.pallas_call(
        _attn_body,
        grid=(b, s // tq),
        in_specs=[
            pl.BlockSpec((1, tq, d), lambda i, j: (i, j, 0)),
            _resident((1, d)),
            pl.BlockSpec((4, tq, MXU_TILE // 2), lambda i, j: (0, j, 0)),
            _resident(ind.shape), _resident(bias.shape), _resident(sink_rows.shape),
            pl.BlockSpec((1, WINDOW, k4.shape[2]), lambda i, j: (i, prev_blk(j), 0)),
            pl.BlockSpec((1, tq, k4.shape[2]), lambda i, j: (i, j, 0)),
            pl.BlockSpec((1, vt.shape[1], WINDOW), lambda i, j: (i, 0, prev_blk(j))),
            pl.BlockSpec((1, vt.shape[1], tq), lambda i, j: (i, 0, j)),
            _resident(wq.shape), _resident(w_o.shape),
        ],
        out_specs=pl.BlockSpec((1, tq, d), lambda i, j: (i, j, 0)),
        out_shape=jax.ShapeDtypeStruct((b, s, d), F32),
        scratch_shapes=[pltpu.VMEM((tq, d), BF16),
                        pltpu.VMEM((nblk, n_kv, ATT_GROUP * WINDOW, MXU_TILE), BF16),
                        pltpu.VMEM((nblk, n_kv, ATT_GROUP // UNIT_HEADS, 2 * WINDOW, UNIT_HEADS * WINDOW), F32),
                        pltpu.VMEM((d, tq), BF16)],
        compiler_params=pltpu.CompilerParams(
            dimension_semantics=("parallel", "parallel"), vmem_limit_bytes=VMEM_LIMIT_BYTES),
        name="swa_attention",
    )(x, g.reshape(1, d), rot, ind, bias, sink_rows, k4, k4, vt, vt, wq, w_o.astype(BF16))


def _ffn_layer(x, g, w_in, w_out, layer):
    b, s, d = x.shape
    return _ffn(x.reshape(b * s, d), g, w_in, w_out, layer).reshape(b, s, d)


def kernel(x, ffn1_norm, ffn1_w_in, ffn1_w_out, mix_norm, ffn2_norm, ffn2_w_in, ffn2_w_out,
           ret_w_in, ret_w_out, kv_norm, kv_w, k_norm, attn_w_q, q_norm, attn_sinks, attn_w_o):
    depth = ffn1_norm.shape[0]
    n_ret = ret_w_in.shape[0]
    k4 = v4 = None
    ffn1_w_in, ffn1_w_out = ffn1_w_in.astype(BF16), ffn1_w_out.astype(BF16)
    ffn2_w_in, ffn2_w_out = ffn2_w_in.astype(BF16), ffn2_w_out.astype(BF16)
    for i in range(depth):
        x = _ffn_layer(x, ffn1_norm[i], ffn1_w_in, ffn1_w_out, i)
        if i < n_ret:
            x = _retention_layer(x, mix_norm[i], ret_w_in[i], ret_w_out[i])
        else:
            j = i - n_ret
            x = _attention_layer(x, mix_norm[i], k4, v4, attn_w_q[j], q_norm[j], attn_sinks[j], attn_w_o[j])
        x = _ffn_layer(x, ffn2_norm[i], ffn2_w_in, ffn2_w_out, i)
        if i == n_ret - 1:
            k4, v4 = _shared_kv(x, kv_norm, kv_w, k_norm)
    return x
```

```python
import functools

import jax
import jax.numpy as jnp
import numpy as np
from jax import lax
from jax.experimental import pallas as pl
from jax.experimental.pallas import tpu as pltpu

F32 = jnp.float32
BF16 = jnp.bfloat16

NORM_EPS = 1e-6
MXU_TILE = 256
VMEM_LIMIT_BYTES = 56 * 1024 * 1024


def _resident(shape):
    return pl.BlockSpec(shape, lambda *_: (0,) * len(shape), pipeline_mode=pl.Buffered(1))


def _rms_scale(x):
    return lax.rsqrt(jnp.mean(x * x, axis=-1, keepdims=True) + NORM_EPS)


FFN_NORM_PIECES = 8


def _ffn_body(x_ref, xnext_ref, g_ref, win_ref, wo_ref, o_ref, h_ref, a_ref, *, ff_chunk):
    tm = x_ref.shape[0]
    d_ff = wo_ref.shape[0]
    n_chunks = d_ff // ff_chunk
    piece = tm // FFN_NORM_PIECES
    step = pl.program_id(0)
    cur, nxt = step % 2, (step + 1) % 2

    def norm_rows(src_ref, slot, rows):
        x = src_ref[rows, :]
        h_ref[slot, rows, :] = (x * _rms_scale(x) * g_ref[...]).astype(BF16)

    @pl.when(step == 0)
    def _():
        norm_rows(x_ref, 0, slice(0, tm))

    for c in range(n_chunks):
        sl = slice(c * ff_chunk, (c + 1) * ff_chunk)
        gate = jnp.dot(h_ref[cur], win_ref[:, sl], preferred_element_type=F32)
        up = jnp.dot(h_ref[cur], win_ref[:, d_ff + c * ff_chunk:d_ff + (c + 1) * ff_chunk],
                     preferred_element_type=F32)
        a_ref[:, sl] = (gate * jax.nn.sigmoid(gate) * up).astype(BF16)
        if c < FFN_NORM_PIECES:
            norm_rows(xnext_ref, nxt, slice(c * piece, (c + 1) * piece))
    y = jnp.dot(a_ref[...], wo_ref[...], preferred_element_type=F32)
    o_ref[...] = x_ref[...] + 0.5 * y


def _layer_resident(stacked, layer):
    return pl.BlockSpec((None,) + stacked.shape[1:], lambda *_: (layer, 0, 0), pipeline_mode=pl.Buffered(1))


def _ffn(x2d, g, w_in, wo, layer, *, tm=1024, ff_chunk=MXU_TILE):
    t, d = x2d.shape
    d_ff = wo.shape[1]
    steps = t // tm
    assert t % tm == 0 and d_ff % ff_chunk == 0 and d_ff // ff_chunk >= FFN_NORM_PIECES
    return pl.pallas_call(
        functools.partial(_ffn_body, ff_chunk=ff_chunk),
        grid=(steps,),
        in_specs=[
            pl.BlockSpec((tm, d), lambda i: (i, 0)),
            pl.BlockSpec((tm, d), lambda i: (jnp.minimum(i + 1, steps - 1), 0)),
            _resident((1, d)),
            _layer_resident(w_in, layer),
            _layer_resident(wo, layer),
        ],
        out_specs=pl.BlockSpec((tm, d), lambda i: (i, 0)),
        out_shape=jax.ShapeDtypeStruct((t, d), F32),
        scratch_shapes=[pltpu.VMEM((2, tm, d), BF16), pltpu.VMEM((tm, d_ff), BF16)],
        compiler_params=pltpu.CompilerParams(
            dimension_semantics=("arbitrary",), vmem_limit_bytes=VMEM_LIMIT_BYTES),
        name="ffn",
    )(x2d, x2d, g.reshape(1, d), w_in, wo)


RET_DK = 256
RET_DV_FACTOR = 2
RET_ROT_BASE = 10000.0
NT_DIMS = (((1,), (1,)), ((), ()))
TN_DIMS = (((0,), (0,)), ((), ()))


def _ret_gamma(h):
    return 1.0 - 2.0 ** (-5.0 - h)


def _rotate_halves(t, cos, sin):
    half = t.shape[1] // 2
    te, to = t[:, :half], t[:, half:]
    return jnp.concatenate([te * cos - to * sin, to * cos + te * sin], axis=1)


def _ret_body(x_ref, g_ref, rot_ref, causal_ref, wqk_ref, wvg_ref, wo_ref,
              o_ref, h_ref, gated_ref, state_ref, *, heads, dk, dv, chunk):
    rows = x_ref.shape[1]

    @pl.when(pl.program_id(1) == 0)
    def _():
        state_ref[...] = jnp.zeros_like(state_ref)

    x = x_ref[0]
    h_ref[...] = (x * _rms_scale(x) * g_ref[...]).astype(BF16)
    for hd in range(heads):
        proj = lambda w_ref, off, width: jnp.dot(
            h_ref[...], w_ref[:, off + hd * width: off + (hd + 1) * width], preferred_element_type=F32)
        qs = _rotate_halves(proj(wqk_ref, 0, dk), rot_ref[0, hd], rot_ref[1, hd]).astype(BF16)
        ks = _rotate_halves(proj(wqk_ref, heads * dk, dk), rot_ref[2, hd], rot_ref[3, hd]).astype(BF16)
        v = proj(wvg_ref, 0, dv).astype(BF16)
        gate = 0.5 * proj(wvg_ref, heads * dv, dv)
        gate = gate + gate * jnp.tanh(gate)
        state = state_ref[hd]
        for c in range(rows // chunk):
            sl = slice(c * chunk, (c + 1) * chunk)
            qc, kc, vc = qs[sl], ks[sl], v[sl]
            scores = lax.dot_general(qc, kc, NT_DIMS, preferred_element_type=F32)
            scores = (scores * causal_ref[...]).astype(BF16)
            out = (jnp.dot(scores, vc, preferred_element_type=F32)
                   + jnp.dot(qc, state.astype(BF16), preferred_element_type=F32))
            state = (state + lax.dot_general(kc, vc, TN_DIMS, preferred_element_type=F32)) * (_ret_gamma(hd) ** chunk)
            out = out * _rms_scale(out)
            gated_ref[sl, hd * dv:(hd + 1) * dv] = (out * gate[sl]).astype(BF16)
        state_ref[hd] = state
    y = jnp.dot(gated_ref[...], wo_ref[...], preferred_element_type=F32)
    o_ref[0] = x_ref[0] + y


def _ret_tables(s, chunk, heads, dk):
    half = dk // 2
    freq = 1.0 / (RET_ROT_BASE ** np.linspace(0.0, 1.0, half))
    ang = np.arange(s)[:, None] * freq[None, :]
    cos, sin = np.cos(ang)[None], np.sin(ang)[None]
    gamma = np.array([_ret_gamma(h) for h in range(heads)])
    assert gamma.min() ** -chunk < 2.0 ** 16, "chunk too long to split the decay into two factors"
    in_chunk = (np.arange(s) % chunk) + 1.0
    q_scale = (gamma[:, None] ** in_chunk[None, :])[:, :, None]
    k_scale = (gamma[:, None] ** -in_chunk[None, :])[:, :, None] * dk ** -0.5
    rot = np.stack([cos * q_scale, sin * q_scale, cos * k_scale, sin * k_scale])
    n = np.arange(chunk)
    causal = (n[:, None] >= n[None, :]).astype(np.float64)
    return jnp.asarray(rot, F32), jnp.asarray(causal, F32)


def _deinterleave_heads(w, heads, dk):
    d = w.shape[0]
    return w.reshape(d, heads, dk // 2, 2).transpose(0, 1, 3, 2).reshape(d, heads * dk)


def _retention_layer(x, g, w_in, w_out, *, chunk=MXU_TILE, ts=2 * MXU_TILE):
    b, s, d = x.shape
    heads = d // RET_DK
    dk, dv = RET_DK, RET_DV_FACTOR * d // heads
    assert s % ts == 0 and ts % chunk == 0
    dq = heads * dk
    wqk = _deinterleave_heads(w_in[:, :2 * dq], 2 * heads, dk).astype(BF16)
    wvg = w_in[:, 2 * dq:].astype(BF16)
    rot, causal = _ret_tables(s, chunk, heads, dk)
    return pl.pallas_call(
        functools.partial(_ret_body, heads=heads, dk=dk, dv=dv, chunk=chunk),
        grid=(b, s // ts),
        in_specs=[
            pl.BlockSpec((1, ts, d), lambda i, j: (i, j, 0)),
            _resident((1, d)),
            pl.BlockSpec((4, heads, ts, dk // 2), lambda i, j: (0, 0, j, 0)),
            _resident(causal.shape),
            _resident(wqk.shape),
            _resident(wvg.shape),
            _resident(w_out.shape),
        ],
        out_specs=pl.BlockSpec((1, ts, d), lambda i, j: (i, j, 0)),
        out_shape=jax.ShapeDtypeStruct((b, s, d), F32),
        scratch_shapes=[pltpu.VMEM((ts, d), BF16), pltpu.VMEM((ts, heads * dv), BF16),
                        pltpu.VMEM((heads, dk, dv), F32)],
        compiler_params=pltpu.CompilerParams(
            dimension_semantics=("parallel", "arbitrary"), vmem_limit_bytes=VMEM_LIMIT_BYTES),
        name="retention",
    )(x, g.reshape(1, d), rot, causal, wqk, wvg, w_out.astype(BF16))


ATT_HD = 64
ATT_GROUP = 8
SLAB_HEADS = MXU_TILE // ATT_HD
WINDOW = 128
ROPE_THETA = 10000.0
NEG_INF = -1e30
LOG2_E = 1.4426950408889634
UNIT_HEADS = 8
ONES_ROWS = 16


def _rope_tables(s, scale):
    half = ATT_HD // 2
    inv_freq = 1.0 / (ROPE_THETA ** (np.arange(0, ATT_HD, 2) / ATT_HD))
    ang = np.arange(s)[:, None] * inv_freq[None, :]
    return (jnp.asarray(np.tile(np.cos(ang), (1, SLAB_HEADS)) * scale, F32),
            jnp.asarray(np.tile(np.sin(ang), (1, SLAB_HEADS)) * scale, F32))


def _slab_gain(g):
    half = ATT_HD // 2
    return jnp.concatenate([jnp.tile(g[:half], SLAB_HEADS), jnp.tile(g[half:], SLAB_HEADS)]).reshape(1, MXU_TILE)


def _kv_body(x_ref, g_ref, kg_ref, cos_ref, sin_ref, wk_ref, wv_ref, k_ref, vt_ref, *, kv_heads):
    x = x_ref[0]
    h = (x * _rms_scale(x) * g_ref[...]).astype(BF16)
    for g in range(kv_heads):
        sl = slice(g * MXU_TILE, (g + 1) * MXU_TILE)
        k = jnp.dot(h, wk_ref[:, sl], preferred_element_type=F32)
        k = k * _rms_scale(k) * kg_ref[...]
        k_ref[0, :, sl] = _rotate_halves(k, cos_ref[...], sin_ref[...]).astype(BF16)
    v = jnp.dot(h, wv_ref[...], preferred_element_type=F32)
    vt_ref[0] = v.T.astype(BF16)


def _shared_kv(x, kv_norm, kv_w, k_norm, *, ts=1024):
    b, s, d = x.shape
    kv_heads = kv_w.shape[1] // (2 * ATT_HD)
    half = ATT_HD // 2
    wk = kv_w[:, :kv_heads * ATT_HD].reshape(d, kv_heads, ATT_HD)
    wk = jnp.concatenate([jnp.tile(wk[:, :, :half], (1, 1, SLAB_HEADS)),
                          jnp.tile(wk[:, :, half:], (1, 1, SLAB_HEADS))], axis=2)
    wk = wk.reshape(d, kv_heads * MXU_TILE).astype(BF16)
    wv = kv_w[:, kv_heads * ATT_HD:].astype(BF16)
    cos, sin = _rope_tables(s, 1.0)
    rot_spec = pl.BlockSpec((ts, MXU_TILE // 2), lambda i, j: (j, 0))
    return pl.pallas_call(
        functools.partial(_kv_body, kv_heads=kv_heads),
        grid=(b, s // ts),
        in_specs=[pl.BlockSpec((1, ts, d), lambda i, j: (i, j, 0)), _resident((1, d)), _resident((1, MXU_TILE)),
                  rot_spec, rot_spec, _resident(wk.shape), _resident(wv.shape)],
        out_specs=[pl.BlockSpec((1, ts, kv_heads * MXU_TILE), lambda i, j: (i, j, 0)),
                   pl.BlockSpec((1, kv_heads * ATT_HD, ts), lambda i, j: (i, 0, j))],
        out_shape=[jax.ShapeDtypeStruct((b, s, kv_heads * MXU_TILE), BF16),
                   jax.ShapeDtypeStruct((b, kv_heads * ATT_HD, s), BF16)],
        compiler_params=pltpu.CompilerParams(
            dimension_semantics=("parallel", "parallel"), vmem_limit_bytes=VMEM_LIMIT_BYTES),
        name="shared_kv",
    )(x, kv_norm.reshape(1, d), _slab_gain(k_norm), cos, sin, wk, wv)


def _attn_body(x_ref, g_ref, rot_ref, ind_ref, bias_ref, sink_ref, kprev_ref, kcur_ref,
               vprev_ref, vcur_ref, wq_ref, wo_ref, o_ref, h_ref, lhs_ref, st_ref, att_ref):
    tq, d = x_ref.shape[1], x_ref.shape[2]
    n_slabs = d // MXU_TILE
    slabs_per_kv = ATT_GROUP // SLAB_HEADS
    n_kv = n_slabs // slabs_per_kv
    n_blk = tq // WINDOW
    x = x_ref[0]
    h_ref[...] = (x * _rms_scale(x) * g_ref[...]).astype(BF16)
    lane = lax.broadcasted_iota(jnp.int32, (WINDOW, MXU_TILE), 1)
    q_lanes = [(lane % (MXU_TILE // 2)) // (ATT_HD // 2) == j for j in range(SLAB_HEADS)]
    first = jnp.where(pl.program_id(1) == 0, 1, 0)

    def project(slab):
        return jnp.dot(h_ref[...], wq_ref[:, slab * MXU_TILE:(slab + 1) * MXU_TILE], preferred_element_type=F32)

    half = MXU_TILE // 2
    q_next = project(0)
    for slab in range(n_slabs):
        q = q_next
        if slab + 1 < n_slabs:
            q_next = project(slab + 1)
        ms = jnp.dot((q * q).astype(BF16), ind_ref[...], preferred_element_type=F32)
        q = q * lax.rsqrt(ms + NORM_EPS)
        q1, q2 = q[:, :half], q[:, half:]
        q = jnp.concatenate([q1 * rot_ref[0] - q2 * rot_ref[1], q2 * rot_ref[2] + q1 * rot_ref[3]], axis=1)
        q = q.astype(BF16)
        kv, sub = divmod(slab, slabs_per_kv)
        for blk in range(n_blk):
            rows = q[blk * WINDOW:(blk + 1) * WINDOW]
            for j in range(SLAB_HEADS):
                r0 = (sub * SLAB_HEADS + j) * WINDOW
                lhs_ref[blk, kv, r0:r0 + WINDOW, :] = jnp.where(q_lanes[j], rows, jnp.zeros_like(rows))

    cols = UNIT_HEADS * WINDOW
    parts = ATT_GROUP // UNIT_HEADS
    for blk in range(n_blk):
        for kv in range(n_kv):
            k_sl = slice(kv * MXU_TILE, (kv + 1) * MXU_TILE)
            if blk == 0:
                keys = jnp.concatenate([kprev_ref[0, :, k_sl], kcur_ref[0, :WINDOW, k_sl]], axis=0)
            else:
                keys = kcur_ref[0, (blk - 1) * WINDOW:(blk + 1) * WINDOW, k_sl]
            for part in range(parts):
                c_sl = slice(part * cols, (part + 1) * cols)
                st_ref[blk, kv, part] = lax.dot_general(
                    keys, lhs_ref[blk, kv, c_sl, :], NT_DIMS, preferred_element_type=F32)
    for blk in range(n_blk):
        for kv in range(n_kv):
            v_sl = slice(kv * ATT_HD, (kv + 1) * ATT_HD)
            if blk == 0:
                vals_t = jnp.concatenate([vprev_ref[0, v_sl, :], vcur_ref[0, v_sl, :WINDOW]], axis=1)
                bias = bias_ref[first]
            else:
                vals_t = vcur_ref[0, v_sl, (blk - 1) * WINDOW:(blk + 1) * WINDOW]
                bias = bias_ref[0]
            vals_t = jnp.concatenate([vals_t, jnp.ones((ONES_ROWS, 2 * WINDOW), BF16)], axis=0)
            for part in range(parts):
                c_sl = slice(part * cols, (part + 1) * cols)
                st = st_ref[blk, kv, part] + bias
                sink = sink_ref[kv, :, c_sl]
                m = jnp.maximum(jnp.max(st, axis=0, keepdims=True), sink)
                p = jnp.exp2(st - m).astype(BF16)
                ot = jnp.dot(vals_t, p, preferred_element_type=F32)
                denom = ot[ATT_HD:ATT_HD + 1, :] + jnp.exp2(sink - m)
                ot = ot[:ATT_HD, :] * (1.0 / denom)
                for j in range(UNIT_HEADS):
                    head = kv * ATT_GROUP + part * UNIT_HEADS + j
                    att_ref[head * ATT_HD:(head + 1) * ATT_HD, blk * WINDOW:(blk + 1) * WINDOW] = (
                        ot[:, j * WINDOW:(j + 1) * WINDOW].astype(BF16))
    y = lax.dot_general(att_ref[...], wo_ref[...], TN_DIMS, preferred_element_type=F32)
    o_ref[0] = x_ref[0] + y


def _attention_layer(x, g, k4, vt, w_q, q_norm, sinks, w_o, *, tq=512):
    b, s, d = x.shape
    heads = d // ATT_HD
    n_slabs = heads // SLAB_HEADS
    half = ATT_HD // 2
    nblk = tq // WINDOW
    assert s % tq == 0
    wq = w_q.reshape(d, n_slabs, SLAB_HEADS, 2, half).transpose(0, 1, 3, 2, 4).reshape(d, d).astype(BF16)
    cos, sin = _rope_tables(s, ATT_HD ** -0.5 * LOG2_E)
    g1, g2 = jnp.tile(q_norm[:half], SLAB_HEADS)[None, :], jnp.tile(q_norm[half:], SLAB_HEADS)[None, :]
    rot = jnp.stack([cos * g1, sin * g2, cos * g2, sin * g1])
    lane = np.arange(MXU_TILE)
    head_of = (lane % (MXU_TILE // 2)) // half
    ind = jnp.asarray((head_of[:, None] == head_of[None, :]) / ATT_HD, BF16)
    kj = np.arange(2 * WINDOW)[:, None]
    qi = np.tile(np.arange(WINDOW), UNIT_HEADS)[None, :]
    band = (kj > qi) & (kj <= qi + WINDOW)
    bias = jnp.asarray(np.stack([np.where(band, 0.0, NEG_INF), np.where(band & (kj >= WINDOW), 0.0, NEG_INF)]), F32)
    n_kv = heads // ATT_GROUP
    sink_rows = jnp.repeat(sinks.astype(F32).reshape(n_kv, ATT_GROUP) * LOG2_E, WINDOW, axis=1)[:, None, :]
    prev_blk = lambda j: jnp.maximum(j * nblk - 1, 0)
    return pl.pallas_call(
        _attn_body,
        grid=(b, s // tq),
        in_specs=[
            pl.BlockSpec((1, tq, d), lambda i, j: (i, j, 0)),
            _resident((1, d)),
            pl.BlockSpec((4, tq, MXU_TILE // 2), lambda i, j: (0, j, 0)),
            _resident(ind.shape), _resident(bias.shape), _resident(sink_rows.shape),
            pl.BlockSpec((1, WINDOW, k4.shape[2]), lambda i, j: (i, prev_blk(j), 0)),
            pl.BlockSpec((1, tq, k4.shape[2]), lambda i, j: (i, j, 0)),
            pl.BlockSpec((1, vt.shape[1], WINDOW), lambda i, j: (i, 0, prev_blk(j))),
            pl.BlockSpec((1, vt.shape[1], tq), lambda i, j: (i, 0, j)),
            _resident(wq.shape), _resident(w_o.shape),
        ],
        out_specs=pl.BlockSpec((1, tq, d), lambda i, j: (i, j, 0)),
        out_shape=jax.ShapeDtypeStruct((b, s, d), F32),
        scratch_shapes=[pltpu.VMEM((tq, d), BF16),
                        pltpu.VMEM((nblk, n_kv, ATT_GROUP * WINDOW, MXU_TILE), BF16),
                        pltpu.VMEM((nblk, n_kv, ATT_GROUP // UNIT_HEADS, 2 * WINDOW, UNIT_HEADS * WINDOW), F32),
                        pltpu.VMEM((d, tq), BF16)],
        compiler_params=pltpu.CompilerParams(
            dimension_semantics=("parallel", "parallel"), vmem_limit_bytes=VMEM_LIMIT_BYTES),
        name="swa_attention",
    )(x, g.reshape(1, d), rot, ind, bias, sink_rows, k4, k4, vt, vt, wq, w_o.astype(BF16))


def _ffn_layer(x, g, w_in, w_out, layer):
    b, s, d = x.shape
    return _ffn(x.reshape(b * s, d), g, w_in, w_out, layer).reshape(b, s, d)


def kernel(x, ffn1_norm, ffn1_w_in, ffn1_w_out, mix_norm, ffn2_norm, ffn2_w_in, ffn2_w_out,
           ret_w_in, ret_w_out, kv_norm, kv_w, k_norm, attn_w_q, q_norm, attn_sinks, attn_w_o):
    depth = ffn1_norm.shape[0]
    n_ret = ret_w_in.shape[0]
    k4 = v4 = None
    ffn1_w_in, ffn1_w_out = ffn1_w_in.astype(BF16), ffn1_w_out.astype(BF16)
    ffn2_w_in, ffn2_w_out = ffn2_w_in.astype(BF16), ffn2_w_out.astype(BF16)
    for i in range(depth):
        x = _ffn_layer(x, ffn1_norm[i], ffn1_w_in, ffn1_w_out, i)
        if i < n_ret:
            x = _retention_layer(x, mix_norm[i], ret_w_in[i], ret_w_out[i])
        else:
            j = i - n_ret
            x = _attention_layer(x, mix_norm[i], k4, v4, attn_w_q[j], q_norm[j], attn_sinks[j], attn_w_o[j])
        x = _ffn_layer(x, ffn2_norm[i], ffn2_w_in, ffn2_w_out, i)
        if i == n_ret - 1:
            k4, v4 = _shared_kv(x, kv_norm, kv_w, k_norm)
    return x
```

```python
import functools

import jax
import jax.numpy as jnp
import numpy as np
from jax import lax
from jax.experimental import pallas as pl
from jax.experimental.pallas import tpu as pltpu

F32 = jnp.float32
BF16 = jnp.bfloat16

NORM_EPS = 1e-6
MXU_TILE = 256
VMEM_LIMIT_BYTES = 56 * 1024 * 1024


def _resident(shape):
    return pl.BlockSpec(shape, lambda *_: (0,) * len(shape), pipeline_mode=pl.Buffered(1))


def _rms_scale(x):
    return lax.rsqrt(jnp.mean(x * x, axis=-1, keepdims=True) + NORM_EPS)


FFN_NORM_PIECES = 8


def _ffn_body(x_ref, xnext_ref, g_ref, win_ref, wo_ref, o_ref, h_ref, a_ref, *, ff_chunk):
    tm = x_ref.shape[0]
    d_ff = wo_ref.shape[0]
    n_chunks = d_ff // ff_chunk
    piece = tm // FFN_NORM_PIECES
    step = pl.program_id(0)
    cur, nxt = step % 2, (step + 1) % 2

    def norm_rows(src_ref, slot, rows):
        x = src_ref[rows, :]
        h_ref[slot, rows, :] = (x * _rms_scale(x) * g_ref[...]).astype(BF16)

    @pl.when(step == 0)
    def _():
        norm_rows(x_ref, 0, slice(0, tm))

    for c in range(n_chunks):
        sl = slice(c * ff_chunk, (c + 1) * ff_chunk)
        gate = jnp.dot(h_ref[cur], win_ref[:, sl], preferred_element_type=F32)
        up = jnp.dot(h_ref[cur], win_ref[:, d_ff + c * ff_chunk:d_ff + (c + 1) * ff_chunk],
                     preferred_element_type=F32)
        a_ref[:, sl] = (gate * jax.nn.sigmoid(gate) * up).astype(BF16)
        if c < FFN_NORM_PIECES:
            norm_rows(xnext_ref, nxt, slice(c * piece, (c + 1) * piece))
    y = jnp.dot(a_ref[...], wo_ref[...], preferred_element_type=F32)
    o_ref[...] = x_ref[...] + 0.5 * y


def _layer_resident(stacked, layer):
    return pl.BlockSpec((None,) + stacked.shape[1:], lambda *_: (layer, 0, 0), pipeline_mode=pl.Buffered(1))


def _ffn(x2d, g, w_in, wo, layer, *, tm=1024, ff_chunk=MXU_TILE):
    t, d = x2d.shape
    d_ff = wo.shape[1]
    steps = t // tm
    assert t % tm == 0 and d_ff % ff_chunk == 0 and d_ff // ff_chunk >= FFN_NORM_PIECES
    return pl.pallas_call(
        functools.partial(_ffn_body, ff_chunk=ff_chunk),
        grid=(steps,),
        in_specs=[
            pl.BlockSpec((tm, d), lambda i: (i, 0)),
            pl.BlockSpec((tm, d), lambda i: (jnp.minimum(i + 1, steps - 1), 0)),
            _resident((1, d)),
            _layer_resident(w_in, layer),
            _layer_resident(wo, layer),
        ],
        out_specs=pl.BlockSpec((tm, d), lambda i: (i, 0)),
        out_shape=jax.ShapeDtypeStruct((t, d), F32),
        scratch_shapes=[pltpu.VMEM((2, tm, d), BF16), pltpu.VMEM((tm, d_ff), BF16)],
        compiler_params=pltpu.CompilerParams(
            dimension_semantics=("arbitrary",), vmem_limit_bytes=VMEM_LIMIT_BYTES),
        name="ffn",
    )(x2d, x2d, g.reshape(1, d), w_in, wo)


RET_DK = 256
RET_DV_FACTOR = 2
RET_ROT_BASE = 10000.0
NT_DIMS = (((1,), (1,)), ((), ()))
TN_DIMS = (((0,), (0,)), ((), ()))


def _ret_gamma(h):
    return 1.0 - 2.0 ** (-5.0 - h)


def _rotate_halves(t, cos, sin):
    half = t.shape[1] // 2
    te, to = t[:, :half], t[:, half:]
    return jnp.concatenate([te * cos - to * sin, to * cos + te * sin], axis=1)


def _ret_body(x_ref, g_ref, rot_ref, causal_ref, wqk_ref, wvg_ref, wo_ref,
              o_ref, h_ref, gated_ref, state_ref, *, heads, dk, dv, chunk):
    rows = x_ref.shape[1]

    @pl.when(pl.program_id(1) == 0)
    def _():
        state_ref[...] = jnp.zeros_like(state_ref)

    x = x_ref[0]
    h_ref[...] = (x * _rms_scale(x) * g_ref[...]).astype(BF16)
    for hd in range(heads):
        proj = lambda w_ref, off, width: jnp.dot(
            h_ref[...], w_ref[:, off + hd * width: off + (hd + 1) * width], preferred_element_type=F32)
        qs = _rotate_halves(proj(wqk_ref, 0, dk), rot_ref[0, hd], rot_ref[1, hd]).astype(BF16)
        ks = _rotate_halves(proj(wqk_ref, heads * dk, dk), rot_ref[2, hd], rot_ref[3, hd]).astype(BF16)
        v = proj(wvg_ref, 0, dv).astype(BF16)
        gate = 0.5 * proj(wvg_ref, heads * dv, dv)
        gate = gate + gate * jnp.tanh(gate)
        state = state_ref[hd]
        for c in range(rows // chunk):
            sl = slice(c * chunk, (c + 1) * chunk)
            qc, kc, vc = qs[sl], ks[sl], v[sl]
            scores = lax.dot_general(qc, kc, NT_DIMS, preferred_element_type=F32)
            scores = (scores * causal_ref[...]).astype(BF16)
            out = (jnp.dot(scores, vc, preferred_element_type=F32)
                   + jnp.dot(qc, state.astype(BF16), preferred_element_type=F32))
            state = (state + lax.dot_general(kc, vc, TN_DIMS, preferred_element_type=F32)) * (_ret_gamma(hd) ** chunk)
            out = out * _rms_scale(out)
            gated_ref[sl, hd * dv:(hd + 1) * dv] = (out * gate[sl]).astype(BF16)
        state_ref[hd] = state
    y = jnp.dot(gated_ref[...], wo_ref[...], preferred_element_type=F32)
    o_ref[0] = x_ref[0] + y


def _ret_tables(s, chunk, heads, dk):
    half = dk // 2
    freq = 1.0 / (RET_ROT_BASE ** np.linspace(0.0, 1.0, half))
    ang = np.arange(s)[:, None] * freq[None, :]
    cos, sin = np.cos(ang)[None], np.sin(ang)[None]
    gamma = np.array([_ret_gamma(h) for h in range(heads)])
    assert gamma.min() ** -chunk < 2.0 ** 16, "chunk too long to split the decay into two factors"
    in_chunk = (np.arange(s) % chunk) + 1.0
    q_scale = (gamma[:, None] ** in_chunk[None, :])[:, :, None]
    k_scale = (gamma[:, None] ** -in_chunk[None, :])[:, :, None] * dk ** -0.5
    rot = np.stack([cos * q_scale, sin * q_scale, cos * k_scale, sin * k_scale])
    n = np.arange(chunk)
    causal = (n[:, None] >= n[None, :]).astype(np.float64)
    return jnp.asarray(rot, F32), jnp.asarray(causal, F32)


def _deinterleave_heads(w, heads, dk):
    d = w.shape[0]
    return w.reshape(d, heads, dk // 2, 2).transpose(0, 1, 3, 2).reshape(d, heads * dk)


def _retention_layer(x, g, w_in, w_out, *, chunk=MXU_TILE, ts=2 * MXU_TILE):
    b, s, d = x.shape
    heads = d // RET_DK
    dk, dv = RET_DK, RET_DV_FACTOR * d // heads
    assert s % ts == 0 and ts % chunk == 0
    dq = heads * dk
    wqk = _deinterleave_heads(w_in[:, :2 * dq], 2 * heads, dk).astype(BF16)
    wvg = w_in[:, 2 * dq:].astype(BF16)
    rot, causal = _ret_tables(s, chunk, heads, dk)
    return pl.pallas_call(
        functools.partial(_ret_body, heads=heads, dk=dk, dv=dv, chunk=chunk),
        grid=(b, s // ts),
        in_specs=[
            pl.BlockSpec((1, ts, d), lambda i, j: (i, j, 0)),
            _resident((1, d)),
            pl.BlockSpec((4, heads, ts, dk // 2), lambda i, j: (0, 0, j, 0)),
            _resident(causal.shape),
            _resident(wqk.shape),
            _resident(wvg.shape),
            _resident(w_out.shape),
        ],
        out_specs=pl.BlockSpec((1, ts, d), lambda i, j: (i, j, 0)),
        out_shape=jax.ShapeDtypeStruct((b, s, d), F32),
        scratch_shapes=[pltpu.VMEM((ts, d), BF16), pltpu.VMEM((ts, heads * dv), BF16),
                        pltpu.VMEM((heads, dk, dv), F32)],
        compiler_params=pltpu.CompilerParams(
            dimension_semantics=("parallel", "arbitrary"), vmem_limit_bytes=VMEM_LIMIT_BYTES),
        name="retention",
    )(x, g.reshape(1, d), rot, causal, wqk, wvg, w_out.astype(BF16))


ATT_HD = 64
ATT_GROUP = 8
SLAB_HEADS = MXU_TILE // ATT_HD
WINDOW = 128
ROPE_THETA = 10000.0
NEG_INF = -1e30
LOG2_E = 1.4426950408889634
UNIT_HEADS = 8
ONES_ROWS = 16


def _rope_tables(s, scale):
    half = ATT_HD // 2
    inv_freq = 1.0 / (ROPE_THETA ** (np.arange(0, ATT_HD, 2) / ATT_HD))
    ang = np.arange(s)[:, None] * inv_freq[None, :]
    return (jnp.asarray(np.tile(np.cos(ang), (1, SLAB_HEADS)) * scale, F32),
            jnp.asarray(np.tile(np.sin(ang), (1, SLAB_HEADS)) * scale, F32))


def _slab_gain(g):
    half = ATT_HD // 2
    return jnp.concatenate([jnp.tile(g[:half], SLAB_HEADS), jnp.tile(g[half:], SLAB_HEADS)]).reshape(1, MXU_TILE)


def _kv_body(x_ref, g_ref, kg_ref, cos_ref, sin_ref, wk_ref, wv_ref, k_ref, vt_ref, *, kv_heads):
    x = x_ref[0]
    h = (x * _rms_scale(x) * g_ref[...]).astype(BF16)
    for g in range(kv_heads):
        sl = slice(g * MXU_TILE, (g + 1) * MXU_TILE)
        k = jnp.dot(h, wk_ref[:, sl], preferred_element_type=F32)
        k = k * _rms_scale(k) * kg_ref[...]
        k_ref[0, :, sl] = _rotate_halves(k, cos_ref[...], sin_ref[...]).astype(BF16)
    v = jnp.dot(h, wv_ref[...], preferred_element_type=F32)
    vt_ref[0] = v.T.astype(BF16)


def _shared_kv(x, kv_norm, kv_w, k_norm, *, ts=1024):
    b, s, d = x.shape
    kv_heads = kv_w.shape[1] // (2 * ATT_HD)
    half = ATT_HD // 2
    wk = kv_w[:, :kv_heads * ATT_HD].reshape(d, kv_heads, ATT_HD)
    wk = jnp.concatenate([jnp.tile(wk[:, :, :half], (1, 1, SLAB_HEADS)),
                          jnp.tile(wk[:, :, half:], (1, 1, SLAB_HEADS))], axis=2)
    wk = wk.reshape(d, kv_heads * MXU_TILE).astype(BF16)
    wv = kv_w[:, kv_heads * ATT_HD:].astype(BF16)
    cos, sin = _rope_tables(s, 1.0)
    rot_spec = pl.BlockSpec((ts, MXU_TILE // 2), lambda i, j: (j, 0))
    return pl.pallas_call(
        functools.partial(_kv_body, kv_heads=kv_heads),
        grid=(b, s // ts),
        in_specs=[pl.BlockSpec((1, ts, d), lambda i, j: (i, j, 0)), _resident((1, d)), _resident((1, MXU_TILE)),
                  rot_spec, rot_spec, _resident(wk.shape), _resident(wv.shape)],
        out_specs=[pl.BlockSpec((1, ts, kv_heads * MXU_TILE), lambda i, j: (i, j, 0)),
                   pl.BlockSpec((1, kv_heads * ATT_HD, ts), lambda i, j: (i, 0, j))],
        out_shape=[jax.ShapeDtypeStruct((b, s, kv_heads * MXU_TILE), BF16),
                   jax.ShapeDtypeStruct((b, kv_heads * ATT_HD, s), BF16)],
        compiler_params=pltpu.CompilerParams(
            dimension_semantics=("parallel", "parallel"), vmem_limit_bytes=VMEM_LIMIT_BYTES),
        name="shared_kv",
    )(x, kv_norm.reshape(1, d), _slab_gain(k_norm), cos, sin, wk, wv)


def _attn_body(x_ref, g_ref, rot_ref, ind_ref, bias_ref, sink_ref, kprev_ref, kcur_ref,
               vprev_ref, vcur_ref, wq_ref, wo_ref, o_ref, h_ref, q_ref, kp_ref, st_ref, att_ref):
    tq, d = x_ref.shape[1], x_ref.shape[2]
    n_slabs = d // MXU_TILE
    slabs_per_kv = ATT_GROUP // SLAB_HEADS
    n_kv = n_slabs // slabs_per_kv
    n_blk = tq // WINDOW
    x = x_ref[0]
    h_ref[...] = (x * _rms_scale(x) * g_ref[...]).astype(BF16)
    lane = lax.broadcasted_iota(jnp.int32, (WINDOW, MXU_TILE), 1)
    head_lanes = [(lane % (MXU_TILE // 2)) // (ATT_HD // 2) == j for j in range(SLAB_HEADS)]
    first = jnp.where(pl.program_id(1) == 0, 1, 0)

    def project(slab):
        return jnp.dot(h_ref[...], wq_ref[:, slab * MXU_TILE:(slab + 1) * MXU_TILE], preferred_element_type=F32)

    half = MXU_TILE // 2
    q_next = project(0)
    for slab in range(n_slabs):
        q = q_next
        if slab + 1 < n_slabs:
            q_next = project(slab + 1)
        ms = jnp.dot((q * q).astype(BF16), ind_ref[...], preferred_element_type=F32)
        q = q * lax.rsqrt(ms + NORM_EPS)
        q1, q2 = q[:, :half], q[:, half:]
        q = jnp.concatenate([q1 * rot_ref[0] - q2 * rot_ref[1], q2 * rot_ref[2] + q1 * rot_ref[3]], axis=1)
        q = q.astype(BF16)
        kv, sub = divmod(slab, slabs_per_kv)
        for blk in range(n_blk):
            q_ref[blk, kv, sub * WINDOW:(sub + 1) * WINDOW, :] = q[blk * WINDOW:(blk + 1) * WINDOW]

    for kv in range(n_kv):
        k_sl = slice(kv * MXU_TILE, (kv + 1) * MXU_TILE)
        for kb in range(n_blk + 1):
            block = kprev_ref[0, :, k_sl] if kb == 0 else kcur_ref[0, (kb - 1) * WINDOW:kb * WINDOW, k_sl]
            for j in range(SLAB_HEADS):
                kp_ref[kv, kb, j] = jnp.where(head_lanes[j], block, jnp.zeros_like(block))

    for blk in range(n_blk):
        for kv in range(n_kv):
            k_stack = jnp.concatenate(
                [kp_ref[kv, blk + side, j] for j in range(SLAB_HEADS) for side in (0, 1)], axis=0)
            st_ref[blk, kv] = lax.dot_general(k_stack, q_ref[blk, kv], NT_DIMS, preferred_element_type=F32)
    keys = 2 * WINDOW
    for blk in range(n_blk):
        for kv in range(n_kv):
            v_sl = slice(kv * ATT_HD, (kv + 1) * ATT_HD)
            if blk == 0:
                vals_t = jnp.concatenate([vprev_ref[0, v_sl, :], vcur_ref[0, v_sl, :WINDOW]], axis=1)
                bias = bias_ref[first]
            else:
                vals_t = vcur_ref[0, v_sl, (blk - 1) * WINDOW:(blk + 1) * WINDOW]
                bias = bias_ref[0]
            vals_t = jnp.concatenate([vals_t, jnp.ones((ONES_ROWS, keys), BF16)], axis=0)
            st = st_ref[blk, kv] + bias
            probs, sink_terms = [], []
            for j in range(SLAB_HEADS):
                sj = st[j * keys:(j + 1) * keys]
                sink = sink_ref[kv, j]
                m = jnp.maximum(jnp.max(sj, axis=0, keepdims=True), sink)
                probs.append(jnp.exp2(sj - m).astype(BF16))
                sink_terms.append(jnp.exp2(sink - m))
            ot = jnp.dot(vals_t, jnp.concatenate(probs, axis=1), preferred_element_type=F32)
            denom = ot[ATT_HD:ATT_HD + 1, :] + jnp.concatenate(sink_terms, axis=1)
            ot = ot[:ATT_HD, :] * (1.0 / denom)
            for j in range(SLAB_HEADS):
                for sub in range(slabs_per_kv):
                    head = kv * ATT_GROUP + sub * SLAB_HEADS + j
                    c0 = (j * slabs_per_kv + sub) * WINDOW
                    att_ref[head * ATT_HD:(head + 1) * ATT_HD, blk * WINDOW:(blk + 1) * WINDOW] = (
                        ot[:, c0:c0 + WINDOW].astype(BF16))
    y = lax.dot_general(att_ref[...], wo_ref[...], TN_DIMS, preferred_element_type=F32)
    o_ref[0] = x_ref[0] + y


def _attention_layer(x, g, k4, vt, w_q, q_norm, sinks, w_o, *, tq=512):
    b, s, d = x.shape
    heads = d // ATT_HD
    n_slabs = heads // SLAB_HEADS
    half = ATT_HD // 2
    nblk = tq // WINDOW
    assert s % tq == 0
    wq = w_q.reshape(d, n_slabs, SLAB_HEADS, 2, half).transpose(0, 1, 3, 2, 4).reshape(d, d).astype(BF16)
    cos, sin = _rope_tables(s, ATT_HD ** -0.5 * LOG2_E)
    g1, g2 = jnp.tile(q_norm[:half], SLAB_HEADS)[None, :], jnp.tile(q_norm[half:], SLAB_HEADS)[None, :]
    rot = jnp.stack([cos * g1, sin * g2, cos * g2, sin * g1])
    lane = np.arange(MXU_TILE)
    head_of = (lane % (MXU_TILE // 2)) // half
    ind = jnp.asarray((head_of[:, None] == head_of[None, :]) / ATT_HD, BF16)
    kj = np.arange(2 * WINDOW)[:, None]
    qi = np.arange(WINDOW)[None, :]
    band = (kj > qi) & (kj <= qi + WINDOW)
    n_kv = heads // ATT_GROUP
    slabs_per_kv = ATT_GROUP // SLAB_HEADS
    tiled = lambda t: np.tile(np.where(t, 0.0, NEG_INF), (SLAB_HEADS, slabs_per_kv))
    bias = jnp.asarray(np.stack([tiled(band), tiled(band & (kj >= WINDOW))]), F32)
    sink_rows = sinks.astype(F32).reshape(n_kv, slabs_per_kv, SLAB_HEADS).transpose(0, 2, 1) * LOG2_E
    sink_rows = jnp.repeat(sink_rows, WINDOW, axis=2)[:, :, None, :]
    prev_blk = lambda j: jnp.maximum(j * nblk - 1, 0)
    return pl.pallas_call(
        _attn_body,
        grid=(b, s // tq),
        in_specs=[
            pl.BlockSpec((1, tq, d), lambda i, j: (i, j, 0)),
            _resident((1, d)),
            pl.BlockSpec((4, tq, MXU_TILE // 2), lambda i, j: (0, j, 0)),
            _resident(ind.shape), _resident(bias.shape), _resident(sink_rows.shape),
            pl.BlockSpec((1, WINDOW, k4.shape[2]), lambda i, j: (i, prev_blk(j), 0)),
            pl.BlockSpec((1, tq, k4.shape[2]), lambda i, j: (i, j, 0)),
            pl.BlockSpec((1, vt.shape[1], WINDOW), lambda i, j: (i, 0, prev_blk(j))),
            pl.BlockSpec((1, vt.shape[1], tq), lambda i, j: (i, 0, j)),
            _resident(wq.shape), _resident(w_o.shape),
        ],
        out_specs=pl.BlockSpec((1, tq, d), lambda i, j: (i, j, 0)),
        out_shape=jax.ShapeDtypeStruct((b, s, d), F32),
        scratch_shapes=[pltpu.VMEM((tq, d), BF16),
                        pltpu.VMEM((nblk, n_kv, slabs_per_kv * WINDOW, MXU_TILE), BF16),
                        pltpu.VMEM((n_kv, nblk + 1, SLAB_HEADS, WINDOW, MXU_TILE), BF16),
                        pltpu.VMEM((nblk, n_kv, SLAB_HEADS * 2 * WINDOW, slabs_per_kv * WINDOW), F32),
                        pltpu.VMEM((d, tq), BF16)],
        compiler_params=pltpu.CompilerParams(
            dimension_semantics=("parallel", "parallel"), vmem_limit_bytes=VMEM_LIMIT_BYTES),
        name="swa_attention",
    )(x, g.reshape(1, d), rot, ind, bias, sink_rows, k4, k4, vt, vt, wq, w_o.astype(BF16))


def _ffn_layer(x, g, w_in, w_out, layer):
    b, s, d = x.shape
    return _ffn(x.reshape(b * s, d), g, w_in, w_out, layer).reshape(b, s, d)


def kernel(x, ffn1_norm, ffn1_w_in, ffn1_w_out, mix_norm, ffn2_norm, ffn2_w_in, ffn2_w_out,
           ret_w_in, ret_w_out, kv_norm, kv_w, k_norm, attn_w_q, q_norm, attn_sinks, attn_w_o):
    depth = ffn1_norm.shape[0]
    n_ret = ret_w_in.shape[0]
    k4 = v4 = None
    ffn1_w_in, ffn1_w_out = ffn1_w_in.astype(BF16), ffn1_w_out.astype(BF16)
    ffn2_w_in, ffn2_w_out = ffn2_w_in.astype(BF16), ffn2_w_out.astype(BF16)
    for i in range(depth):
        x = _ffn_layer(x, ffn1_norm[i], ffn1_w_in, ffn1_w_out, i)
        if i < n_ret:
            x = _retention_layer(x, mix_norm[i], ret_w_in[i], ret_w_out[i])
        else:
            j = i - n_ret
            x = _attention_layer(x, mix_norm[i], k4, v4, attn_w_q[j], q_norm[j], attn_sinks[j], attn_w_o[j])
        x = _ffn_layer(x, ffn2_norm[i], ffn2_w_in, ffn2_w_out, i)
        if i == n_ret - 1:
            k4, v4 = _shared_kv(x, kv_norm, kv_w, k_norm)
    return x
```

```python
import functools

import jax
import jax.numpy as jnp
import numpy as np
from jax import lax
from jax.experimental import pallas as pl
from jax.experimental.pallas import tpu as pltpu

F32 = jnp.float32
BF16 = jnp.bfloat16

NORM_EPS = 1e-6
MXU_TILE = 256
VMEM_LIMIT_BYTES = 56 * 1024 * 1024


def _resident(shape):
    return pl.BlockSpec(shape, lambda *_: (0,) * len(shape), pipeline_mode=pl.Buffered(1))


def _rms_scale(x):
    return lax.rsqrt(jnp.mean(x * x, axis=-1, keepdims=True) + NORM_EPS)


FFN_NORM_PIECES = 8


def _ffn_body(x_ref, xnext_ref, g_ref, win_ref, wo_ref, o_ref, h_ref, a_ref, *, ff_chunk):
    tm = x_ref.shape[0]
    d_ff = wo_ref.shape[0]
    n_chunks = d_ff // ff_chunk
    piece = tm // FFN_NORM_PIECES
    step = pl.program_id(0)
    cur, nxt = step % 2, (step + 1) % 2

    def norm_rows(src_ref, slot, rows):
        x = src_ref[rows, :]
        h_ref[slot, rows, :] = (x * _rms_scale(x) * g_ref[...]).astype(BF16)

    @pl.when(step == 0)
    def _():
        norm_rows(x_ref, 0, slice(0, tm))

    for c in range(n_chunks):
        sl = slice(c * ff_chunk, (c + 1) * ff_chunk)
        gate = jnp.dot(h_ref[cur], win_ref[:, sl], preferred_element_type=F32)
        up = jnp.dot(h_ref[cur], win_ref[:, d_ff + c * ff_chunk:d_ff + (c + 1) * ff_chunk],
                     preferred_element_type=F32)
        a_ref[:, sl] = (gate * jax.nn.sigmoid(gate) * up).astype(BF16)
        if c < FFN_NORM_PIECES:
            norm_rows(xnext_ref, nxt, slice(c * piece, (c + 1) * piece))
    y = jnp.dot(a_ref[...], wo_ref[...], preferred_element_type=F32)
    o_ref[...] = x_ref[...] + 0.5 * y


def _layer_resident(stacked, layer):
    return pl.BlockSpec((None,) + stacked.shape[1:], lambda *_: (layer, 0, 0), pipeline_mode=pl.Buffered(1))


def _ffn(x2d, g, w_in, wo, layer, *, tm=1024, ff_chunk=MXU_TILE):
    t, d = x2d.shape
    d_ff = wo.shape[1]
    steps = t // tm
    assert t % tm == 0 and d_ff % ff_chunk == 0 and d_ff // ff_chunk >= FFN_NORM_PIECES
    return pl.pallas_call(
        functools.partial(_ffn_body, ff_chunk=ff_chunk),
        grid=(steps,),
        in_specs=[
            pl.BlockSpec((tm, d), lambda i: (i, 0)),
            pl.BlockSpec((tm, d), lambda i: (jnp.minimum(i + 1, steps - 1), 0)),
            _resident((1, d)),
            _layer_resident(w_in, layer),
            _layer_resident(wo, layer),
        ],
        out_specs=pl.BlockSpec((tm, d), lambda i: (i, 0)),
        out_shape=jax.ShapeDtypeStruct((t, d), F32),
        scratch_shapes=[pltpu.VMEM((2, tm, d), BF16), pltpu.VMEM((tm, d_ff), BF16)],
        compiler_params=pltpu.CompilerParams(
            dimension_semantics=("arbitrary",), vmem_limit_bytes=VMEM_LIMIT_BYTES),
        name="ffn",
    )(x2d, x2d, g.reshape(1, d), w_in, wo)


RET_DK = 256
RET_DV_FACTOR = 2
RET_ROT_BASE = 10000.0
NT_DIMS = (((1,), (1,)), ((), ()))
TN_DIMS = (((0,), (0,)), ((), ()))


def _ret_gamma(h):
    return 1.0 - 2.0 ** (-5.0 - h)


def _rotate_halves(t, cos, sin):
    half = t.shape[1] // 2
    te, to = t[:, :half], t[:, half:]
    return jnp.concatenate([te * cos - to * sin, to * cos + te * sin], axis=1)


def _ret_body(x_ref, g_ref, rot_ref, causal_ref, wqk_ref, wvg_ref, wo_ref,
              o_ref, h_ref, gated_ref, state_ref, *, heads, dk, dv, chunk):
    rows = x_ref.shape[1]

    @pl.when(pl.program_id(1) == 0)
    def _():
        state_ref[...] = jnp.zeros_like(state_ref)

    x = x_ref[0]
    h_ref[...] = (x * _rms_scale(x) * g_ref[...]).astype(BF16)
    for hd in range(heads):
        proj = lambda w_ref, off, width: jnp.dot(
            h_ref[...], w_ref[:, off + hd * width: off + (hd + 1) * width], preferred_element_type=F32)
        qs = _rotate_halves(proj(wqk_ref, 0, dk), rot_ref[0, hd], rot_ref[1, hd]).astype(BF16)
        ks = _rotate_halves(proj(wqk_ref, heads * dk, dk), rot_ref[2, hd], rot_ref[3, hd]).astype(BF16)
        v = proj(wvg_ref, 0, dv).astype(BF16)
        gate = 0.5 * proj(wvg_ref, heads * dv, dv)
        gate = gate + gate * jnp.tanh(gate)
        state = state_ref[hd]
        for c in range(rows // chunk):
            sl = slice(c * chunk, (c + 1) * chunk)
            qc, kc, vc = qs[sl], ks[sl], v[sl]
            scores = lax.dot_general(qc, kc, NT_DIMS, preferred_element_type=F32)
            scores = (scores * causal_ref[...]).astype(BF16)
            out = (jnp.dot(scores, vc, preferred_element_type=F32)
                   + jnp.dot(qc, state.astype(BF16), preferred_element_type=F32))
            state = (state + lax.dot_general(kc, vc, TN_DIMS, preferred_element_type=F32)) * (_ret_gamma(hd) ** chunk)
            out = out * _rms_scale(out)
            gated_ref[sl, hd * dv:(hd + 1) * dv] = (out * gate[sl]).astype(BF16)
        state_ref[hd] = state
    y = jnp.dot(gated_ref[...], wo_ref[...], preferred_element_type=F32)
    o_ref[0] = x_ref[0] + y


def _ret_tables(s, chunk, heads, dk):
    half = dk // 2
    freq = 1.0 / (RET_ROT_BASE ** np.linspace(0.0, 1.0, half))
    ang = np.arange(s)[:, None] * freq[None, :]
    cos, sin = np.cos(ang)[None], np.sin(ang)[None]
    gamma = np.array([_ret_gamma(h) for h in range(heads)])
    assert gamma.min() ** -chunk < 2.0 ** 16, "chunk too long to split the decay into two factors"
    in_chunk = (np.arange(s) % chunk) + 1.0
    q_scale = (gamma[:, None] ** in_chunk[None, :])[:, :, None]
    k_scale = (gamma[:, None] ** -in_chunk[None, :])[:, :, None] * dk ** -0.5
    rot = np.stack([cos * q_scale, sin * q_scale, cos * k_scale, sin * k_scale])
    n = np.arange(chunk)
    causal = (n[:, None] >= n[None, :]).astype(np.float64)
    return jnp.asarray(rot, F32), jnp.asarray(causal, F32)


def _deinterleave_heads(w, heads, dk):
    d = w.shape[0]
    return w.reshape(d, heads, dk // 2, 2).transpose(0, 1, 3, 2).reshape(d, heads * dk)


def _retention_layer(x, g, w_in, w_out, *, chunk=MXU_TILE, ts=2 * MXU_TILE):
    b, s, d = x.shape
    heads = d // RET_DK
    dk, dv = RET_DK, RET_DV_FACTOR * d // heads
    assert s % ts == 0 and ts % chunk == 0
    dq = heads * dk
    wqk = _deinterleave_heads(w_in[:, :2 * dq], 2 * heads, dk).astype(BF16)
    wvg = w_in[:, 2 * dq:].astype(BF16)
    rot, causal = _ret_tables(s, chunk, heads, dk)
    return pl.pallas_call(
        functools.partial(_ret_body, heads=heads, dk=dk, dv=dv, chunk=chunk),
        grid=(b, s // ts),
        in_specs=[
            pl.BlockSpec((1, ts, d), lambda i, j: (i, j, 0)),
            _resident((1, d)),
            pl.BlockSpec((4, heads, ts, dk // 2), lambda i, j: (0, 0, j, 0)),
            _resident(causal.shape),
            _resident(wqk.shape),
            _resident(wvg.shape),
            _resident(w_out.shape),
        ],
        out_specs=pl.BlockSpec((1, ts, d), lambda i, j: (i, j, 0)),
        out_shape=jax.ShapeDtypeStruct((b, s, d), F32),
        scratch_shapes=[pltpu.VMEM((ts, d), BF16), pltpu.VMEM((ts, heads * dv), BF16),
                        pltpu.VMEM((heads, dk, dv), F32)],
        compiler_params=pltpu.CompilerParams(
            dimension_semantics=("parallel", "arbitrary"), vmem_limit_bytes=VMEM_LIMIT_BYTES),
        name="retention",
    )(x, g.reshape(1, d), rot, causal, wqk, wvg, w_out.astype(BF16))


ATT_HD = 64
ATT_GROUP = 8
SLAB_HEADS = MXU_TILE // ATT_HD
WINDOW = 128
ROPE_THETA = 10000.0
NEG_INF = -1e30
LOG2_E = 1.4426950408889634
UNIT_HEADS = 8
ONES_ROWS = 16


def _rope_tables(s, scale):
    half = ATT_HD // 2
    inv_freq = 1.0 / (ROPE_THETA ** (np.arange(0, ATT_HD, 2) / ATT_HD))
    ang = np.arange(s)[:, None] * inv_freq[None, :]
    return (jnp.asarray(np.tile(np.cos(ang), (1, SLAB_HEADS)) * scale, F32),
            jnp.asarray(np.tile(np.sin(ang), (1, SLAB_HEADS)) * scale, F32))


def _slab_gain(g):
    half = ATT_HD // 2
    return jnp.concatenate([jnp.tile(g[:half], SLAB_HEADS), jnp.tile(g[half:], SLAB_HEADS)]).reshape(1, MXU_TILE)


def _kv_body(x_ref, g_ref, kg_ref, cos_ref, sin_ref, wk_ref, wv_ref, k_ref, vt_ref, *, kv_heads):
    x = x_ref[0]
    h = (x * _rms_scale(x) * g_ref[...]).astype(BF16)
    for g in range(kv_heads):
        sl = slice(g * MXU_TILE, (g + 1) * MXU_TILE)
        k = jnp.dot(h, wk_ref[:, sl], preferred_element_type=F32)
        k = k * _rms_scale(k) * kg_ref[...]
        k_ref[0, :, sl] = _rotate_halves(k, cos_ref[...], sin_ref[...]).astype(BF16)
    v = jnp.dot(h, wv_ref[...], preferred_element_type=F32)
    vt_ref[0] = v.T.astype(BF16)


def _shared_kv(x, kv_norm, kv_w, k_norm, *, ts=1024):
    b, s, d = x.shape
    kv_heads = kv_w.shape[1] // (2 * ATT_HD)
    half = ATT_HD // 2
    wk = kv_w[:, :kv_heads * ATT_HD].reshape(d, kv_heads, ATT_HD)
    wk = jnp.concatenate([jnp.tile(wk[:, :, :half], (1, 1, SLAB_HEADS)),
                          jnp.tile(wk[:, :, half:], (1, 1, SLAB_HEADS))], axis=2)
    wk = wk.reshape(d, kv_heads * MXU_TILE).astype(BF16)
    wv = kv_w[:, kv_heads * ATT_HD:].astype(BF16)
    cos, sin = _rope_tables(s, 1.0)
    rot_spec = pl.BlockSpec((ts, MXU_TILE // 2), lambda i, j: (j, 0))
    return pl.pallas_call(
        functools.partial(_kv_body, kv_heads=kv_heads),
        grid=(b, s // ts),
        in_specs=[pl.BlockSpec((1, ts, d), lambda i, j: (i, j, 0)), _resident((1, d)), _resident((1, MXU_TILE)),
                  rot_spec, rot_spec, _resident(wk.shape), _resident(wv.shape)],
        out_specs=[pl.BlockSpec((1, ts, kv_heads * MXU_TILE), lambda i, j: (i, j, 0)),
                   pl.BlockSpec((1, kv_heads * ATT_HD, ts), lambda i, j: (i, 0, j))],
        out_shape=[jax.ShapeDtypeStruct((b, s, kv_heads * MXU_TILE), BF16),
                   jax.ShapeDtypeStruct((b, kv_heads * ATT_HD, s), BF16)],
        compiler_params=pltpu.CompilerParams(
            dimension_semantics=("parallel", "parallel"), vmem_limit_bytes=VMEM_LIMIT_BYTES),
        name="shared_kv",
    )(x, kv_norm.reshape(1, d), _slab_gain(k_norm), cos, sin, wk, wv)


def _attn_body(x_ref, g_ref, rot_ref, ind_ref, bias_ref, sink_ref, kprev_ref, kcur_ref,
               vprev_ref, vcur_ref, wq_ref, wo_ref, o_ref, h_ref, lhs_ref, st_ref, att_ref):
    tq, d = x_ref.shape[1], x_ref.shape[2]
    n_slabs = d // MXU_TILE
    slabs_per_kv = ATT_GROUP // SLAB_HEADS
    n_kv = n_slabs // slabs_per_kv
    n_blk = tq // WINDOW
    x = x_ref[0]
    h_ref[...] = (x * _rms_scale(x) * g_ref[...]).astype(BF16)
    lane = lax.broadcasted_iota(jnp.int32, (WINDOW, MXU_TILE), 1)
    q_lanes = [(lane % (MXU_TILE // 2)) // (ATT_HD // 2) == j for j in range(SLAB_HEADS)]
    first = jnp.where(pl.program_id(1) == 0, 1, 0)

    def project(slab):
        return jnp.dot(h_ref[...], wq_ref[:, slab * MXU_TILE:(slab + 1) * MXU_TILE], preferred_element_type=F32)

    half = MXU_TILE // 2
    q_next = project(0)
    for slab in range(n_slabs):
        q = q_next
        if slab + 1 < n_slabs:
            q_next = project(slab + 1)
        ms = jnp.dot((q * q).astype(BF16), ind_ref[...], preferred_element_type=F32)
        q = q * lax.rsqrt(ms + NORM_EPS)
        q1, q2 = q[:, :half], q[:, half:]
        q = jnp.concatenate([q1 * rot_ref[0] - q2 * rot_ref[1], q2 * rot_ref[2] + q1 * rot_ref[3]], axis=1)
        q = q.astype(BF16)
        kv, sub = divmod(slab, slabs_per_kv)
        for blk in range(n_blk):
            rows = q[blk * WINDOW:(blk + 1) * WINDOW]
            for j in range(SLAB_HEADS):
                r0 = (sub * SLAB_HEADS + j) * WINDOW
                lhs_ref[blk, kv, r0:r0 + WINDOW, :] = jnp.where(q_lanes[j], rows, jnp.zeros_like(rows))

    cols = UNIT_HEADS * WINDOW
    parts = ATT_GROUP // UNIT_HEADS
    for blk in range(n_blk):
        for kv in range(n_kv):
            k_sl = slice(kv * MXU_TILE, (kv + 1) * MXU_TILE)
            if blk == 0:
                keys = jnp.concatenate([kprev_ref[0, :, k_sl], kcur_ref[0, :WINDOW, k_sl]], axis=0)
            else:
                keys = kcur_ref[0, (blk - 1) * WINDOW:(blk + 1) * WINDOW, k_sl]
            for part in range(parts):
                c_sl = slice(part * cols, (part + 1) * cols)
                st_ref[blk, kv, part] = lax.dot_general(
                    keys, lhs_ref[blk, kv, c_sl, :], NT_DIMS, preferred_element_type=F32)
    for blk in range(n_blk):
        for kv in range(n_kv):
            v_sl = slice(kv * ATT_HD, (kv + 1) * ATT_HD)
            if blk == 0:
                vals_t = jnp.concatenate([vprev_ref[0, v_sl, :], vcur_ref[0, v_sl, :WINDOW]], axis=1)
                bias = jnp.tile(bias_ref[first], (1, UNIT_HEADS))
            else:
                vals_t = vcur_ref[0, v_sl, (blk - 1) * WINDOW:(blk + 1) * WINDOW]
                bias = jnp.tile(bias_ref[0], (1, UNIT_HEADS))
            vals_t = jnp.concatenate([vals_t, jnp.ones((ONES_ROWS, 2 * WINDOW), BF16)], axis=0)
            for part in range(parts):
                c_sl = slice(part * cols, (part + 1) * cols)
                st = st_ref[blk, kv, part] + bias
                sink = sink_ref[kv, :, c_sl]
                m = jnp.maximum(jnp.max(st, axis=0, keepdims=True), sink)
                p = jnp.exp2(st - m).astype(BF16)
                ot = jnp.dot(vals_t, p, preferred_element_type=F32)
                denom = ot[ATT_HD:ATT_HD + 1, :] + jnp.exp2(sink - m)
                ot = ot[:ATT_HD, :] * (1.0 / denom)
                for j in range(UNIT_HEADS):
                    head = kv * ATT_GROUP + part * UNIT_HEADS + j
                    att_ref[head * ATT_HD:(head + 1) * ATT_HD, blk * WINDOW:(blk + 1) * WINDOW] = (
                        ot[:, j * WINDOW:(j + 1) * WINDOW].astype(BF16))
    y = lax.dot_general(att_ref[...], wo_ref[...], TN_DIMS, preferred_element_type=F32)
    o_ref[0] = x_ref[0] + y


def _attention_layer(x, g, k4, vt, w_q, q_norm, sinks, w_o, *, tq=512):
    b, s, d = x.shape
    heads = d // ATT_HD
    n_slabs = heads // SLAB_HEADS
    half = ATT_HD // 2
    nblk = tq // WINDOW
    assert s % tq == 0
    wq = w_q.reshape(d, n_slabs, SLAB_HEADS, 2, half).transpose(0, 1, 3, 2, 4).reshape(d, d).astype(BF16)
    cos, sin = _rope_tables(s, ATT_HD ** -0.5 * LOG2_E)
    g1, g2 = jnp.tile(q_norm[:half], SLAB_HEADS)[None, :], jnp.tile(q_norm[half:], SLAB_HEADS)[None, :]
    rot = jnp.stack([cos * g1, sin * g2, cos * g2, sin * g1])
    lane = np.arange(MXU_TILE)
    head_of = (lane % (MXU_TILE // 2)) // half
    ind = jnp.asarray((head_of[:, None] == head_of[None, :]) / ATT_HD, BF16)
    kj = np.arange(2 * WINDOW)[:, None]
    qi = np.arange(WINDOW)[None, :]
    band = (kj > qi) & (kj <= qi + WINDOW)
    bias = jnp.asarray(np.stack([np.where(band, 0.0, NEG_INF), np.where(band & (kj >= WINDOW), 0.0, NEG_INF)]), F32)
    n_kv = heads // ATT_GROUP
    sink_rows = jnp.repeat(sinks.astype(F32).reshape(n_kv, ATT_GROUP) * LOG2_E, WINDOW, axis=1)[:, None, :]
    prev_blk = lambda j: jnp.maximum(j * nblk - 1, 0)
    return pl.pallas_call(
        _attn_body,
        grid=(b, s // tq),
        in_specs=[
            pl.BlockSpec((1, tq, d), lambda i, j: (i, j, 0)),
            _resident((1, d)),
            pl.BlockSpec((4, tq, MXU_TILE // 2), lambda i, j: (0, j, 0)),
            _resident(ind.shape), _resident(bias.shape), _resident(sink_rows.shape),
            pl.BlockSpec((1, WINDOW, k4.shape[2]), lambda i, j: (i, prev_blk(j), 0)),
            pl.BlockSpec((1, tq, k4.shape[2]), lambda i, j: (i, j, 0)),
            pl.BlockSpec((1, vt.shape[1], WINDOW), lambda i, j: (i, 0, prev_blk(j))),
            pl.BlockSpec((1, vt.shape[1], tq), lambda i, j: (i, 0, j)),
            _resident(wq.shape), _resident(w_o.shape),
        ],
        out_specs=pl.BlockSpec((1, tq, d), lambda i, j: (i, j, 0)),
        out_shape=jax.ShapeDtypeStruct((b, s, d), F32),
        scratch_shapes=[pltpu.VMEM((tq, d), BF16),
                        pltpu.VMEM((nblk, n_kv, ATT_GROUP * WINDOW, MXU_TILE), BF16),
                        pltpu.VMEM((nblk, n_kv, ATT_GROUP // UNIT_HEADS, 2 * WINDOW, UNIT_HEADS * WINDOW), F32),
                        pltpu.VMEM((d, tq), BF16)],
        compiler_params=pltpu.CompilerParams(
            dimension_semantics=("parallel", "parallel"), vmem_limit_bytes=VMEM_LIMIT_BYTES),
        name="swa_attention",
    )(x, g.reshape(1, d), rot, ind, bias, sink_rows, k4, k4, vt, vt, wq, w_o.astype(BF16))


def _ffn_layer(x, g, w_in, w_out, layer):
    b, s, d = x.shape
    return _ffn(x.reshape(b * s, d), g, w_in, w_out, layer).reshape(b, s, d)


def kernel(x, ffn1_norm, ffn1_w_in, ffn1_w_out, mix_norm, ffn2_norm, ffn2_w_in, ffn2_w_out,
           ret_w_in, ret_w_out, kv_norm, kv_w, k_norm, attn_w_q, q_norm, attn_sinks, attn_w_o):
    depth = ffn1_norm.shape[0]
    n_ret = ret_w_in.shape[0]
    k4 = v4 = None
    ffn1_w_in, ffn1_w_out = ffn1_w_in.astype(BF16), ffn1_w_out.astype(BF16)
    ffn2_w_in, ffn2_w_out = ffn2_w_in.astype(BF16), ffn2_w_out.astype(BF16)
    for i in range(depth):
        x = _ffn_layer(x, ffn1_norm[i], ffn1_w_in, ffn1_w_out, i)
        if i < n_ret:
            x = _retention_layer(x, mix_norm[i], ret_w_in[i], ret_w_out[i])
        else:
            j = i - n_ret
            x = _attention_layer(x, mix_norm[i], k4, v4, attn_w_q[j], q_norm[j], attn_sinks[j], attn_w_o[j])
        x = _ffn_layer(x, ffn2_norm[i], ffn2_w_in, ffn2_w_out, i)
        if i == n_ret - 1:
            k4, v4 = _shared_kv(x, kv_norm, kv_w, k_norm)
    return x
```

```python
import functools

import jax
import jax.numpy as jnp
import numpy as np
from jax import lax
from jax.experimental import pallas as pl
from jax.experimental.pallas import tpu as pltpu

F32 = jnp.float32
BF16 = jnp.bfloat16

NORM_EPS = 1e-6
MXU_TILE = 256
VMEM_LIMIT_BYTES = 56 * 1024 * 1024


def _resident(shape):
    return pl.BlockSpec(shape, lambda *_: (0,) * len(shape), pipeline_mode=pl.Buffered(1))


def _rms_scale(x):
    return lax.rsqrt(jnp.mean(x * x, axis=-1, keepdims=True) + NORM_EPS)


FFN_NORM_PIECES = 8


def _ffn_body(x_ref, xnext_ref, g_ref, win_ref, wo_ref, o_ref, h_ref, a_ref, *, ff_chunk):
    tm = x_ref.shape[0]
    d_ff = wo_ref.shape[0]
    n_chunks = d_ff // ff_chunk
    piece = tm // FFN_NORM_PIECES
    step = pl.program_id(0)
    cur, nxt = step % 2, (step + 1) % 2

    def norm_rows(src_ref, slot, rows):
        x = src_ref[rows, :]
        h_ref[slot, rows, :] = (x * _rms_scale(x) * g_ref[...]).astype(BF16)

    @pl.when(step == 0)
    def _():
        norm_rows(x_ref, 0, slice(0, tm))

    for c in range(n_chunks):
        sl = slice(c * ff_chunk, (c + 1) * ff_chunk)
        gate = jnp.dot(h_ref[cur], win_ref[:, sl], preferred_element_type=F32)
        up = jnp.dot(h_ref[cur], win_ref[:, d_ff + c * ff_chunk:d_ff + (c + 1) * ff_chunk],
                     preferred_element_type=F32)
        a_ref[:, sl] = (gate * jax.nn.sigmoid(gate) * up).astype(BF16)
        if c < FFN_NORM_PIECES:
            norm_rows(xnext_ref, nxt, slice(c * piece, (c + 1) * piece))
    y = jnp.dot(a_ref[...], wo_ref[...], preferred_element_type=F32)
    o_ref[...] = x_ref[...] + 0.5 * y


def _layer_resident(stacked, layer):
    return pl.BlockSpec((None,) + stacked.shape[1:], lambda *_: (layer, 0, 0), pipeline_mode=pl.Buffered(1))


def _ffn(x2d, g, w_in, wo, layer, *, tm=1024, ff_chunk=MXU_TILE):
    t, d = x2d.shape
    d_ff = wo.shape[1]
    steps = t // tm
    assert t % tm == 0 and d_ff % ff_chunk == 0 and d_ff // ff_chunk >= FFN_NORM_PIECES
    return pl.pallas_call(
        functools.partial(_ffn_body, ff_chunk=ff_chunk),
        grid=(steps,),
        in_specs=[
            pl.BlockSpec((tm, d), lambda i: (i, 0)),
            pl.BlockSpec((tm, d), lambda i: (jnp.minimum(i + 1, steps - 1), 0)),
            _resident((1, d)),
            _layer_resident(w_in, layer),
            _layer_resident(wo, layer),
        ],
        out_specs=pl.BlockSpec((tm, d), lambda i: (i, 0)),
        out_shape=jax.ShapeDtypeStruct((t, d), F32),
        scratch_shapes=[pltpu.VMEM((2, tm, d), BF16), pltpu.VMEM((tm, d_ff), BF16)],
        compiler_params=pltpu.CompilerParams(
            dimension_semantics=("arbitrary",), vmem_limit_bytes=VMEM_LIMIT_BYTES),
        name="ffn",
    )(x2d, x2d, g.reshape(1, d), w_in, wo)


RET_DK = 256
RET_DV_FACTOR = 2
RET_ROT_BASE = 10000.0
NT_DIMS = (((1,), (1,)), ((), ()))
TN_DIMS = (((0,), (0,)), ((), ()))


def _ret_gamma(h):
    return 1.0 - 2.0 ** (-5.0 - h)


def _rotate_halves(t, cos, sin):
    half = t.shape[1] // 2
    te, to = t[:, :half], t[:, half:]
    return jnp.concatenate([te * cos - to * sin, to * cos + te * sin], axis=1)


def _ret_body(x_ref, g_ref, rot_ref, causal_ref, wqk_ref, wvg_ref, wo_ref,
              o_ref, h_ref, gated_ref, state_ref, *, heads, dk, dv, chunk):
    rows = x_ref.shape[1]

    @pl.when(pl.program_id(1) == 0)
    def _():
        state_ref[...] = jnp.zeros_like(state_ref)

    x = x_ref[0]
    h_ref[...] = (x * _rms_scale(x) * g_ref[...]).astype(BF16)
    for hd in range(heads):
        proj = lambda w_ref, off, width: jnp.dot(
            h_ref[...], w_ref[:, off + hd * width: off + (hd + 1) * width], preferred_element_type=F32)
        qs = _rotate_halves(proj(wqk_ref, 0, dk), rot_ref[0, hd], rot_ref[1, hd]).astype(BF16)
        ks = _rotate_halves(proj(wqk_ref, heads * dk, dk), rot_ref[2, hd], rot_ref[3, hd]).astype(BF16)
        v = proj(wvg_ref, 0, dv).astype(BF16)
        gate = 0.5 * proj(wvg_ref, heads * dv, dv)
        gate = gate + gate * jnp.tanh(gate)
        state = state_ref[hd]
        for c in range(rows // chunk):
            sl = slice(c * chunk, (c + 1) * chunk)
            qc, kc, vc = qs[sl], ks[sl], v[sl]
            scores = lax.dot_general(qc, kc, NT_DIMS, preferred_element_type=F32)
            scores = (scores * causal_ref[...]).astype(BF16)
            out = (jnp.dot(scores, vc, preferred_element_type=F32)
                   + jnp.dot(qc, state.astype(BF16), preferred_element_type=F32))
            state = (state + lax.dot_general(kc, vc, TN_DIMS, preferred_element_type=F32)) * (_ret_gamma(hd) ** chunk)
            out = out * _rms_scale(out)
            gated_ref[sl, hd * dv:(hd + 1) * dv] = (out * gate[sl]).astype(BF16)
        state_ref[hd] = state
    y = jnp.dot(gated_ref[...], wo_ref[...], preferred_element_type=F32)
    o_ref[0] = x_ref[0] + y


def _ret_tables(s, chunk, heads, dk):
    half = dk // 2
    freq = 1.0 / (RET_ROT_BASE ** np.linspace(0.0, 1.0, half))
    ang = np.arange(s)[:, None] * freq[None, :]
    cos, sin = np.cos(ang)[None], np.sin(ang)[None]
    gamma = np.array([_ret_gamma(h) for h in range(heads)])
    assert gamma.min() ** -chunk < 2.0 ** 16, "chunk too long to split the decay into two factors"
    in_chunk = (np.arange(s) % chunk) + 1.0
    q_scale = (gamma[:, None] ** in_chunk[None, :])[:, :, None]
    k_scale = (gamma[:, None] ** -in_chunk[None, :])[:, :, None] * dk ** -0.5
    rot = np.stack([cos * q_scale, sin * q_scale, cos * k_scale, sin * k_scale])
    n = np.arange(chunk)
    causal = (n[:, None] >= n[None, :]).astype(np.float64)
    return jnp.asarray(rot, F32), jnp.asarray(causal, F32)


def _deinterleave_heads(w, heads, dk):
    d = w.shape[0]
    return w.reshape(d, heads, dk // 2, 2).transpose(0, 1, 3, 2).reshape(d, heads * dk)


def _retention_layer(x, g, w_in, w_out, *, chunk=MXU_TILE, ts=2 * MXU_TILE):
    b, s, d = x.shape
    heads = d // RET_DK
    dk, dv = RET_DK, RET_DV_FACTOR * d // heads
    assert s % ts == 0 and ts % chunk == 0
    dq = heads * dk
    wqk = _deinterleave_heads(w_in[:, :2 * dq], 2 * heads, dk).astype(BF16)
    wvg = w_in[:, 2 * dq:].astype(BF16)
    rot, causal = _ret_tables(s, chunk, heads, dk)
    return pl.pallas_call(
        functools.partial(_ret_body, heads=heads, dk=dk, dv=dv, chunk=chunk),
        grid=(b, s // ts),
        in_specs=[
            pl.BlockSpec((1, ts, d), lambda i, j: (i, j, 0)),
            _resident((1, d)),
            pl.BlockSpec((4, heads, ts, dk // 2), lambda i, j: (0, 0, j, 0)),
            _resident(causal.shape),
            _resident(wqk.shape),
            _resident(wvg.shape),
            _resident(w_out.shape),
        ],
        out_specs=pl.BlockSpec((1, ts, d), lambda i, j: (i, j, 0)),
        out_shape=jax.ShapeDtypeStruct((b, s, d), F32),
        scratch_shapes=[pltpu.VMEM((ts, d), BF16), pltpu.VMEM((ts, heads * dv), BF16),
                        pltpu.VMEM((heads, dk, dv), F32)],
        compiler_params=pltpu.CompilerParams(
            dimension_semantics=("arbitrary", "arbitrary"), vmem_limit_bytes=VMEM_LIMIT_BYTES),
        name="retention",
    )(x, g.reshape(1, d), rot, causal, wqk, wvg, w_out.astype(BF16))


ATT_HD = 64
ATT_GROUP = 8
SLAB_HEADS = MXU_TILE // ATT_HD
WINDOW = 128
ROPE_THETA = 10000.0
NEG_INF = -1e30
LOG2_E = 1.4426950408889634
UNIT_HEADS = 8
ONES_ROWS = 16


def _rope_tables(s, scale):
    half = ATT_HD // 2
    inv_freq = 1.0 / (ROPE_THETA ** (np.arange(0, ATT_HD, 2) / ATT_HD))
    ang = np.arange(s)[:, None] * inv_freq[None, :]
    return (jnp.asarray(np.tile(np.cos(ang), (1, SLAB_HEADS)) * scale, F32),
            jnp.asarray(np.tile(np.sin(ang), (1, SLAB_HEADS)) * scale, F32))


def _slab_gain(g):
    half = ATT_HD // 2
    return jnp.concatenate([jnp.tile(g[:half], SLAB_HEADS), jnp.tile(g[half:], SLAB_HEADS)]).reshape(1, MXU_TILE)


def _kv_body(x_ref, g_ref, kg_ref, cos_ref, sin_ref, wk_ref, wv_ref, k_ref, vt_ref, *, kv_heads):
    x = x_ref[0]
    h = (x * _rms_scale(x) * g_ref[...]).astype(BF16)
    for g in range(kv_heads):
        sl = slice(g * MXU_TILE, (g + 1) * MXU_TILE)
        k = jnp.dot(h, wk_ref[:, sl], preferred_element_type=F32)
        k = k * _rms_scale(k) * kg_ref[...]
        k_ref[0, :, sl] = _rotate_halves(k, cos_ref[...], sin_ref[...]).astype(BF16)
    v = jnp.dot(h, wv_ref[...], preferred_element_type=F32)
    vt_ref[0] = v.T.astype(BF16)


def _shared_kv(x, kv_norm, kv_w, k_norm, *, ts=1024):
    b, s, d = x.shape
    kv_heads = kv_w.shape[1] // (2 * ATT_HD)
    half = ATT_HD // 2
    wk = kv_w[:, :kv_heads * ATT_HD].reshape(d, kv_heads, ATT_HD)
    wk = jnp.concatenate([jnp.tile(wk[:, :, :half], (1, 1, SLAB_HEADS)),
                          jnp.tile(wk[:, :, half:], (1, 1, SLAB_HEADS))], axis=2)
    wk = wk.reshape(d, kv_heads * MXU_TILE).astype(BF16)
    wv = kv_w[:, kv_heads * ATT_HD:].astype(BF16)
    cos, sin = _rope_tables(s, 1.0)
    rot_spec = pl.BlockSpec((ts, MXU_TILE // 2), lambda i, j: (j, 0))
    return pl.pallas_call(
        functools.partial(_kv_body, kv_heads=kv_heads),
        grid=(b, s // ts),
        in_specs=[pl.BlockSpec((1, ts, d), lambda i, j: (i, j, 0)), _resident((1, d)), _resident((1, MXU_TILE)),
                  rot_spec, rot_spec, _resident(wk.shape), _resident(wv.shape)],
        out_specs=[pl.BlockSpec((1, ts, kv_heads * MXU_TILE), lambda i, j: (i, j, 0)),
                   pl.BlockSpec((1, kv_heads * ATT_HD, ts), lambda i, j: (i, 0, j))],
        out_shape=[jax.ShapeDtypeStruct((b, s, kv_heads * MXU_TILE), BF16),
                   jax.ShapeDtypeStruct((b, kv_heads * ATT_HD, s), BF16)],
        compiler_params=pltpu.CompilerParams(
            dimension_semantics=("arbitrary", "arbitrary"), vmem_limit_bytes=VMEM_LIMIT_BYTES),
        name="shared_kv",
    )(x, kv_norm.reshape(1, d), _slab_gain(k_norm), cos, sin, wk, wv)


def _attn_body(x_ref, g_ref, rot_ref, ind_ref, bias_ref, sink_ref, kprev_ref, kcur_ref,
               vprev_ref, vcur_ref, wq_ref, wo_ref, o_ref, h_ref, lhs_ref, st_ref, att_ref):
    tq, d = x_ref.shape[1], x_ref.shape[2]
    n_slabs = d // MXU_TILE
    slabs_per_kv = ATT_GROUP // SLAB_HEADS
    n_kv = n_slabs // slabs_per_kv
    n_blk = tq // WINDOW
    x = x_ref[0]
    h_ref[...] = (x * _rms_scale(x) * g_ref[...]).astype(BF16)
    lane = lax.broadcasted_iota(jnp.int32, (WINDOW, MXU_TILE), 1)
    q_lanes = [(lane % (MXU_TILE // 2)) // (ATT_HD // 2) == j for j in range(SLAB_HEADS)]
    first = jnp.where(pl.program_id(1) == 0, 1, 0)

    def project(slab):
        return jnp.dot(h_ref[...], wq_ref[:, slab * MXU_TILE:(slab + 1) * MXU_TILE], preferred_element_type=F32)

    half = MXU_TILE // 2
    q_next = project(0)
    for slab in range(n_slabs):
        q = q_next
        if slab + 1 < n_slabs:
            q_next = project(slab + 1)
        ms = jnp.dot((q * q).astype(BF16), ind_ref[...], preferred_element_type=F32)
        q = q * lax.rsqrt(ms + NORM_EPS)
        q1, q2 = q[:, :half], q[:, half:]
        q = jnp.concatenate([q1 * rot_ref[0] - q2 * rot_ref[1], q2 * rot_ref[2] + q1 * rot_ref[3]], axis=1)
        q = q.astype(BF16)
        kv, sub = divmod(slab, slabs_per_kv)
        for blk in range(n_blk):
            rows = q[blk * WINDOW:(blk + 1) * WINDOW]
            for j in range(SLAB_HEADS):
                r0 = (sub * SLAB_HEADS + j) * WINDOW
                lhs_ref[blk, kv, r0:r0 + WINDOW, :] = jnp.where(q_lanes[j], rows, jnp.zeros_like(rows))

    cols = UNIT_HEADS * WINDOW
    parts = ATT_GROUP // UNIT_HEADS
    for blk in range(n_blk):
        for kv in range(n_kv):
            k_sl = slice(kv * MXU_TILE, (kv + 1) * MXU_TILE)
            if blk == 0:
                keys = jnp.concatenate([kprev_ref[0, :, k_sl], kcur_ref[0, :WINDOW, k_sl]], axis=0)
            else:
                keys = kcur_ref[0, (blk - 1) * WINDOW:(blk + 1) * WINDOW, k_sl]
            for part in range(parts):
                c_sl = slice(part * cols, (part + 1) * cols)
                st_ref[blk, kv, part] = lax.dot_general(
                    keys, lhs_ref[blk, kv, c_sl, :], NT_DIMS, preferred_element_type=F32)
    for blk in range(n_blk):
        for kv in range(n_kv):
            v_sl = slice(kv * ATT_HD, (kv + 1) * ATT_HD)
            if blk == 0:
                vals_t = jnp.concatenate([vprev_ref[0, v_sl, :], vcur_ref[0, v_sl, :WINDOW]], axis=1)
                bias = jnp.tile(bias_ref[first], (1, UNIT_HEADS))
            else:
                vals_t = vcur_ref[0, v_sl, (blk - 1) * WINDOW:(blk + 1) * WINDOW]
                bias = jnp.tile(bias_ref[0], (1, UNIT_HEADS))
            vals_t = jnp.concatenate([vals_t, jnp.ones((ONES_ROWS, 2 * WINDOW), BF16)], axis=0)
            for part in range(parts):
                c_sl = slice(part * cols, (part + 1) * cols)
                st = st_ref[blk, kv, part] + bias
                sink = sink_ref[kv, :, c_sl]
                m = jnp.maximum(jnp.max(st, axis=0, keepdims=True), sink)
                p = jnp.exp2(st - m).astype(BF16)
                ot = jnp.dot(vals_t, p, preferred_element_type=F32)
                denom = ot[ATT_HD:ATT_HD + 1, :] + jnp.exp2(sink - m)
                ot = ot[:ATT_HD, :] * (1.0 / denom)
                for j in range(UNIT_HEADS):
                    head = kv * ATT_GROUP + part * UNIT_HEADS + j
                    att_ref[head * ATT_HD:(head + 1) * ATT_HD, blk * WINDOW:(blk + 1) * WINDOW] = (
                        ot[:, j * WINDOW:(j + 1) * WINDOW].astype(BF16))
    y = lax.dot_general(att_ref[...], wo_ref[...], TN_DIMS, preferred_element_type=F32)
    o_ref[0] = x_ref[0] + y


def _attention_layer(x, g, k4, vt, w_q, q_norm, sinks, w_o, *, tq=512):
    b, s, d = x.shape
    heads = d // ATT_HD
    n_slabs = heads // SLAB_HEADS
    half = ATT_HD // 2
    nblk = tq // WINDOW
    assert s % tq == 0
    wq = w_q.reshape(d, n_slabs, SLAB_HEADS, 2, half).transpose(0, 1, 3, 2, 4).reshape(d, d).astype(BF16)
    cos, sin = _rope_tables(s, ATT_HD ** -0.5 * LOG2_E)
    g1, g2 = jnp.tile(q_norm[:half], SLAB_HEADS)[None, :], jnp.tile(q_norm[half:], SLAB_HEADS)[None, :]
    rot = jnp.stack([cos * g1, sin * g2, cos * g2, sin * g1])
    lane = np.arange(MXU_TILE)
    head_of = (lane % (MXU_TILE // 2)) // half
    ind = jnp.asarray((head_of[:, None] == head_of[None, :]) / ATT_HD, BF16)
    kj = np.arange(2 * WINDOW)[:, None]
    qi = np.arange(WINDOW)[None, :]
    band = (kj > qi) & (kj <= qi + WINDOW)
    bias = jnp.asarray(np.stack([np.where(band, 0.0, NEG_INF), np.where(band & (kj >= WINDOW), 0.0, NEG_INF)]), F32)
    n_kv = heads // ATT_GROUP
    sink_rows = jnp.repeat(sinks.astype(F32).reshape(n_kv, ATT_GROUP) * LOG2_E, WINDOW, axis=1)[:, None, :]
    prev_blk = lambda j: jnp.maximum(j * nblk - 1, 0)
    return pl.pallas_call(
        _attn_body,
        grid=(b, s // tq),
        in_specs=[
            pl.BlockSpec((1, tq, d), lambda i, j: (i, j, 0)),
            _resident((1, d)),
            pl.BlockSpec((4, tq, MXU_TILE // 2), lambda i, j: (0, j, 0)),
            _resident(ind.shape), _resident(bias.shape), _resident(sink_rows.shape),
            pl.BlockSpec((1, WINDOW, k4.shape[2]), lambda i, j: (i, prev_blk(j), 0)),
            pl.BlockSpec((1, tq, k4.shape[2]), lambda i, j: (i, j, 0)),
            pl.BlockSpec((1, vt.shape[1], WINDOW), lambda i, j: (i, 0, prev_blk(j))),
            pl.BlockSpec((1, vt.shape[1], tq), lambda i, j: (i, 0, j)),
            _resident(wq.shape), _resident(w_o.shape),
        ],
        out_specs=pl.BlockSpec((1, tq, d), lambda i, j: (i, j, 0)),
        out_shape=jax.ShapeDtypeStruct((b, s, d), F32),
        scratch_shapes=[pltpu.VMEM((tq, d), BF16),
                        pltpu.VMEM((nblk, n_kv, ATT_GROUP * WINDOW, MXU_TILE), BF16),
                        pltpu.VMEM((nblk, n_kv, ATT_GROUP // UNIT_HEADS, 2 * WINDOW, UNIT_HEADS * WINDOW), F32),
                        pltpu.VMEM((d, tq), BF16)],
        compiler_params=pltpu.CompilerParams(
            dimension_semantics=("arbitrary", "arbitrary"), vmem_limit_bytes=VMEM_LIMIT_BYTES),
        name="swa_attention",
    )(x, g.reshape(1, d), rot, ind, bias, sink_rows, k4, k4, vt, vt, wq, w_o.astype(BF16))


def _ffn_layer(x, g, w_in, w_out, layer):
    b, s, d = x.shape
    return _ffn(x.reshape(b * s, d), g, w_in, w_out, layer).reshape(b, s, d)


def kernel(x, ffn1_norm, ffn1_w_in, ffn1_w_out, mix_norm, ffn2_norm, ffn2_w_in, ffn2_w_out,
           ret_w_in, ret_w_out, kv_norm, kv_w, k_norm, attn_w_q, q_norm, attn_sinks, attn_w_o):
    depth = ffn1_norm.shape[0]
    n_ret = ret_w_in.shape[0]
    k4 = v4 = None
    ffn1_w_in, ffn1_w_out = ffn1_w_in.astype(BF16), ffn1_w_out.astype(BF16)
    ffn2_w_in, ffn2_w_out = ffn2_w_in.astype(BF16), ffn2_w_out.astype(BF16)
    for i in range(depth):
        x = _ffn_layer(x, ffn1_norm[i], ffn1_w_in, ffn1_w_out, i)
        if i < n_ret:
            x = _retention_layer(x, mix_norm[i], ret_w_in[i], ret_w_out[i])
        else:
            j = i - n_ret
            x = _attention_layer(x, mix_norm[i], k4, v4, attn_w_q[j], q_norm[j], attn_sinks[j], attn_w_o[j])
        x = _ffn_layer(x, ffn2_norm[i], ffn2_w_in, ffn2_w_out, i)
        if i == n_ret - 1:
            k4, v4 = _shared_kv(x, kv_norm, kv_w, k_norm)
    return x
```

```python
import functools

import jax
import jax.numpy as jnp
import numpy as np
from jax import lax
from jax.experimental import pallas as pl
from jax.experimental.pallas import tpu as pltpu

F32 = jnp.float32
BF16 = jnp.bfloat16

NORM_EPS = 1e-6
MXU_TILE = 256
VMEM_LIMIT_BYTES = 56 * 1024 * 1024


def _resident(shape):
    return pl.BlockSpec(shape, lambda *_: (0,) * len(shape), pipeline_mode=pl.Buffered(1))


def _rms_scale(x):
    return lax.rsqrt(jnp.mean(x * x, axis=-1, keepdims=True) + NORM_EPS)


FFN_NORM_PIECES = 8


def _ffn_body(x_ref, xnext_ref, g_ref, win_ref, wo_ref, o_ref, h_ref, a_ref, *, ff_chunk):
    tm = x_ref.shape[0]
    d_ff = wo_ref.shape[0]
    n_chunks = d_ff // ff_chunk
    piece = tm // FFN_NORM_PIECES
    step = pl.program_id(0)
    cur, nxt = step % 2, (step + 1) % 2

    def norm_rows(src_ref, slot, rows):
        x = src_ref[rows, :]
        h_ref[slot, rows, :] = (x * _rms_scale(x) * g_ref[...]).astype(BF16)

    @pl.when(step == 0)
    def _():
        norm_rows(x_ref, 0, slice(0, tm))

    for c in range(n_chunks):
        sl = slice(c * ff_chunk, (c + 1) * ff_chunk)
        gate = jnp.dot(h_ref[cur], win_ref[:, sl], preferred_element_type=F32)
        up = jnp.dot(h_ref[cur], win_ref[:, d_ff + c * ff_chunk:d_ff + (c + 1) * ff_chunk],
                     preferred_element_type=F32)
        a_ref[:, sl] = (gate * jax.nn.sigmoid(gate) * up).astype(BF16)
        if c < FFN_NORM_PIECES:
            norm_rows(xnext_ref, nxt, slice(c * piece, (c + 1) * piece))
    y = jnp.dot(a_ref[...], wo_ref[...], preferred_element_type=F32)
    o_ref[...] = x_ref[...] + 0.5 * y


def _layer_resident(stacked, layer):
    return pl.BlockSpec((None,) + stacked.shape[1:], lambda *_: (layer, 0, 0), pipeline_mode=pl.Buffered(1))


def _ffn(x2d, g, w_in, wo, layer, *, tm=1024, ff_chunk=MXU_TILE):
    t, d = x2d.shape
    d_ff = wo.shape[1]
    steps = t // tm
    assert t % tm == 0 and d_ff % ff_chunk == 0 and d_ff // ff_chunk >= FFN_NORM_PIECES
    return pl.pallas_call(
        functools.partial(_ffn_body, ff_chunk=ff_chunk),
        grid=(steps,),
        in_specs=[
            pl.BlockSpec((tm, d), lambda i: (i, 0)),
            pl.BlockSpec((tm, d), lambda i: (jnp.minimum(i + 1, steps - 1), 0)),
            _resident((1, d)),
            _layer_resident(w_in, layer),
            _layer_resident(wo, layer),
        ],
        out_specs=pl.BlockSpec((tm, d), lambda i: (i, 0)),
        out_shape=jax.ShapeDtypeStruct((t, d), F32),
        scratch_shapes=[pltpu.VMEM((2, tm, d), BF16), pltpu.VMEM((tm, d_ff), BF16)],
        compiler_params=pltpu.CompilerParams(
            dimension_semantics=("arbitrary",), vmem_limit_bytes=VMEM_LIMIT_BYTES),
        name="ffn",
    )(x2d, x2d, g.reshape(1, d), w_in, wo)


RET_DK = 256
RET_DV_FACTOR = 2
RET_ROT_BASE = 10000.0
NT_DIMS = (((1,), (1,)), ((), ()))
TN_DIMS = (((0,), (0,)), ((), ()))


def _ret_gamma(h):
    return 1.0 - 2.0 ** (-5.0 - h)


def _rotate_halves(t, cos, sin):
    half = t.shape[1] // 2
    te, to = t[:, :half], t[:, half:]
    return jnp.concatenate([te * cos - to * sin, to * cos + te * sin], axis=1)


def _ret_body(x_ref, rot_ref, causal_ref, wqk_ref, wvg_ref, wo_ref,
              o_ref, h_ref, gated_ref, state_ref, *, heads, dk, dv, chunk):
    rows = x_ref.shape[1]

    @pl.when(pl.program_id(1) == 0)
    def _():
        state_ref[...] = jnp.zeros_like(state_ref)

    x = x_ref[0]
    h_ref[...] = (x * _rms_scale(x)).astype(BF16)
    for hd in range(heads):
        proj = lambda w_ref, off, width: jnp.dot(
            h_ref[...], w_ref[:, off + hd * width: off + (hd + 1) * width], preferred_element_type=F32)
        qs = _rotate_halves(proj(wqk_ref, 0, dk), rot_ref[0, hd], rot_ref[1, hd]).astype(BF16)
        ks = _rotate_halves(proj(wqk_ref, heads * dk, dk), rot_ref[2, hd], rot_ref[3, hd]).astype(BF16)
        v = proj(wvg_ref, 0, dv).astype(BF16)
        gate = 0.5 * proj(wvg_ref, heads * dv, dv)
        gate = gate + gate * jnp.tanh(gate)
        state = state_ref[hd]
        for c in range(rows // chunk):
            sl = slice(c * chunk, (c + 1) * chunk)
            qc, kc, vc = qs[sl], ks[sl], v[sl]
            scores = lax.dot_general(qc, kc, NT_DIMS, preferred_element_type=F32)
            scores = (scores * causal_ref[...]).astype(BF16)
            out = (jnp.dot(scores, vc, preferred_element_type=F32)
                   + jnp.dot(qc, state.astype(BF16), preferred_element_type=F32))
            state = (state + lax.dot_general(kc, vc, TN_DIMS, preferred_element_type=F32)) * (_ret_gamma(hd) ** chunk)
            out = out * _rms_scale(out)
            gated_ref[sl, hd * dv:(hd + 1) * dv] = (out * gate[sl]).astype(BF16)
        state_ref[hd] = state
    y = jnp.dot(gated_ref[...], wo_ref[...], preferred_element_type=F32)
    o_ref[0] = x_ref[0] + y


def _ret_tables(s, chunk, heads, dk):
    half = dk // 2
    freq = 1.0 / (RET_ROT_BASE ** np.linspace(0.0, 1.0, half))
    ang = np.arange(s)[:, None] * freq[None, :]
    cos, sin = np.cos(ang)[None], np.sin(ang)[None]
    gamma = np.array([_ret_gamma(h) for h in range(heads)])
    assert gamma.min() ** -chunk < 2.0 ** 16, "chunk too long to split the decay into two factors"
    in_chunk = (np.arange(s) % chunk) + 1.0
    q_scale = (gamma[:, None] ** in_chunk[None, :])[:, :, None]
    k_scale = (gamma[:, None] ** -in_chunk[None, :])[:, :, None] * dk ** -0.5
    rot = np.stack([cos * q_scale, sin * q_scale, cos * k_scale, sin * k_scale])
    n = np.arange(chunk)
    causal = (n[:, None] >= n[None, :]).astype(np.float64)
    return jnp.asarray(rot, F32), jnp.asarray(causal, F32)


def _deinterleave_heads(w, heads, dk):
    d = w.shape[0]
    return w.reshape(d, heads, dk // 2, 2).transpose(0, 1, 3, 2).reshape(d, heads * dk)


def _retention_layer(x, g, w_in, w_out, *, chunk=MXU_TILE, ts=2 * MXU_TILE):
    b, s, d = x.shape
    heads = d // RET_DK
    dk, dv = RET_DK, RET_DV_FACTOR * d // heads
    assert s % ts == 0 and ts % chunk == 0
    dq = heads * dk
    w_in = w_in * g[:, None]
    wqk = _deinterleave_heads(w_in[:, :2 * dq], 2 * heads, dk).astype(BF16)
    wvg = w_in[:, 2 * dq:].astype(BF16)
    rot, causal = _ret_tables(s, chunk, heads, dk)
    return pl.pallas_call(
        functools.partial(_ret_body, heads=heads, dk=dk, dv=dv, chunk=chunk),
        grid=(b, s // ts),
        in_specs=[
            pl.BlockSpec((1, ts, d), lambda i, j: (i, j, 0)),
            pl.BlockSpec((4, heads, ts, dk // 2), lambda i, j: (0, 0, j, 0)),
            _resident(causal.shape),
            _resident(wqk.shape),
            _resident(wvg.shape),
            _resident(w_out.shape),
        ],
        out_specs=pl.BlockSpec((1, ts, d), lambda i, j: (i, j, 0)),
        out_shape=jax.ShapeDtypeStruct((b, s, d), F32),
        scratch_shapes=[pltpu.VMEM((ts, d), BF16), pltpu.VMEM((ts, heads * dv), BF16),
                        pltpu.VMEM((heads, dk, dv), F32)],
        compiler_params=pltpu.CompilerParams(
            dimension_semantics=("parallel", "arbitrary"), vmem_limit_bytes=VMEM_LIMIT_BYTES),
        name="retention",
    )(x, rot, causal, wqk, wvg, w_out.astype(BF16))


ATT_HD = 64
ATT_GROUP = 8
SLAB_HEADS = MXU_TILE // ATT_HD
WINDOW = 128
ROPE_THETA = 10000.0
NEG_INF = -1e30
LOG2_E = 1.4426950408889634
UNIT_HEADS = 8
ONES_ROWS = 16


def _rope_tables(s, scale):
    half = ATT_HD // 2
    inv_freq = 1.0 / (ROPE_THETA ** (np.arange(0, ATT_HD, 2) / ATT_HD))
    ang = np.arange(s)[:, None] * inv_freq[None, :]
    return (jnp.asarray(np.tile(np.cos(ang), (1, SLAB_HEADS)) * scale, F32),
            jnp.asarray(np.tile(np.sin(ang), (1, SLAB_HEADS)) * scale, F32))


def _slab_gain(g):
    half = ATT_HD // 2
    return jnp.concatenate([jnp.tile(g[:half], SLAB_HEADS), jnp.tile(g[half:], SLAB_HEADS)]).reshape(1, MXU_TILE)


def _kv_body(x_ref, g_ref, kg_ref, cos_ref, sin_ref, wk_ref, wv_ref, k_ref, vt_ref, *, kv_heads):
    x = x_ref[0]
    h = (x * _rms_scale(x) * g_ref[...]).astype(BF16)
    for g in range(kv_heads):
        sl = slice(g * MXU_TILE, (g + 1) * MXU_TILE)
        k = jnp.dot(h, wk_ref[:, sl], preferred_element_type=F32)
        k = k * _rms_scale(k) * kg_ref[...]
        k_ref[0, :, sl] = _rotate_halves(k, cos_ref[...], sin_ref[...]).astype(BF16)
    v = jnp.dot(h, wv_ref[...], preferred_element_type=F32)
    vt_ref[0] = v.T.astype(BF16)


def _shared_kv(x, kv_norm, kv_w, k_norm, *, ts=1024):
    b, s, d = x.shape
    kv_heads = kv_w.shape[1] // (2 * ATT_HD)
    half = ATT_HD // 2
    wk = kv_w[:, :kv_heads * ATT_HD].reshape(d, kv_heads, ATT_HD)
    wk = jnp.concatenate([jnp.tile(wk[:, :, :half], (1, 1, SLAB_HEADS)),
                          jnp.tile(wk[:, :, half:], (1, 1, SLAB_HEADS))], axis=2)
    wk = wk.reshape(d, kv_heads * MXU_TILE).astype(BF16)
    wv = kv_w[:, kv_heads * ATT_HD:].astype(BF16)
    cos, sin = _rope_tables(s, 1.0)
    rot_spec = pl.BlockSpec((ts, MXU_TILE // 2), lambda i, j: (j, 0))
    return pl.pallas_call(
        functools.partial(_kv_body, kv_heads=kv_heads),
        grid=(b, s // ts),
        in_specs=[pl.BlockSpec((1, ts, d), lambda i, j: (i, j, 0)), _resident((1, d)), _resident((1, MXU_TILE)),
                  rot_spec, rot_spec, _resident(wk.shape), _resident(wv.shape)],
        out_specs=[pl.BlockSpec((1, ts, kv_heads * MXU_TILE), lambda i, j: (i, j, 0)),
                   pl.BlockSpec((1, kv_heads * ATT_HD, ts), lambda i, j: (i, 0, j))],
        out_shape=[jax.ShapeDtypeStruct((b, s, kv_heads * MXU_TILE), BF16),
                   jax.ShapeDtypeStruct((b, kv_heads * ATT_HD, s), BF16)],
        compiler_params=pltpu.CompilerParams(
            dimension_semantics=("parallel", "parallel"), vmem_limit_bytes=VMEM_LIMIT_BYTES),
        name="shared_kv",
    )(x, kv_norm.reshape(1, d), _slab_gain(k_norm), cos, sin, wk, wv)


def _attn_body(x_ref, rot_ref, ind_ref, bias_ref, sink_ref, kprev_ref, kcur_ref,
               vprev_ref, vcur_ref, wq_ref, wo_ref, o_ref, h_ref, lhs_ref, st_ref, att_ref):
    tq, d = x_ref.shape[1], x_ref.shape[2]
    n_slabs = d // MXU_TILE
    slabs_per_kv = ATT_GROUP // SLAB_HEADS
    n_kv = n_slabs // slabs_per_kv
    n_blk = tq // WINDOW
    x = x_ref[0]
    h_ref[...] = (x * _rms_scale(x)).astype(BF16)
    lane = lax.broadcasted_iota(jnp.int32, (WINDOW, MXU_TILE), 1)
    q_lanes = [(lane % (MXU_TILE // 2)) // (ATT_HD // 2) == j for j in range(SLAB_HEADS)]
    first = jnp.where(pl.program_id(1) == 0, 1, 0)

    def project(slab):
        return jnp.dot(h_ref[...], wq_ref[:, slab * MXU_TILE:(slab + 1) * MXU_TILE], preferred_element_type=F32)

    half = MXU_TILE // 2
    q_next = project(0)
    for slab in range(n_slabs):
        q = q_next
        if slab + 1 < n_slabs:
            q_next = project(slab + 1)
        ms = jnp.dot((q * q).astype(BF16), ind_ref[...], preferred_element_type=F32)
        q = q * lax.rsqrt(ms + NORM_EPS)
        q1, q2 = q[:, :half], q[:, half:]
        q = jnp.concatenate([q1 * rot_ref[0] - q2 * rot_ref[1], q2 * rot_ref[2] + q1 * rot_ref[3]], axis=1)
        q = q.astype(BF16)
        kv, sub = divmod(slab, slabs_per_kv)
        for blk in range(n_blk):
            rows = q[blk * WINDOW:(blk + 1) * WINDOW]
            for j in range(SLAB_HEADS):
                r0 = (sub * SLAB_HEADS + j) * WINDOW
                lhs_ref[blk, kv, r0:r0 + WINDOW, :] = jnp.where(q_lanes[j], rows, jnp.zeros_like(rows))

    cols = UNIT_HEADS * WINDOW
    parts = ATT_GROUP // UNIT_HEADS
    for blk in range(n_blk):
        for kv in range(n_kv):
            k_sl = slice(kv * MXU_TILE, (kv + 1) * MXU_TILE)
            if blk == 0:
                keys = jnp.concatenate([kprev_ref[0, :, k_sl], kcur_ref[0, :WINDOW, k_sl]], axis=0)
            else:
                keys = kcur_ref[0, (blk - 1) * WINDOW:(blk + 1) * WINDOW, k_sl]
            for part in range(parts):
                c_sl = slice(part * cols, (part + 1) * cols)
                st_ref[blk, kv, part] = lax.dot_general(
                    keys, lhs_ref[blk, kv, c_sl, :], NT_DIMS, preferred_element_type=F32)
    for blk in range(n_blk):
        for kv in range(n_kv):
            v_sl = slice(kv * ATT_HD, (kv + 1) * ATT_HD)
            if blk == 0:
                vals_t = jnp.concatenate([vprev_ref[0, v_sl, :], vcur_ref[0, v_sl, :WINDOW]], axis=1)
                bias = jnp.tile(bias_ref[first], (1, UNIT_HEADS))
            else:
                vals_t = vcur_ref[0, v_sl, (blk - 1) * WINDOW:(blk + 1) * WINDOW]
                bias = jnp.tile(bias_ref[0], (1, UNIT_HEADS))
            vals_t = jnp.concatenate([vals_t, jnp.ones((ONES_ROWS, 2 * WINDOW), BF16)], axis=0)
            for part in range(parts):
                c_sl = slice(part * cols, (part + 1) * cols)
                st = st_ref[blk, kv, part] + bias
                sink = sink_ref[kv, :, c_sl]
                m = jnp.maximum(jnp.max(st, axis=0, keepdims=True), sink)
                p = jnp.exp2(st - m).astype(BF16)
                ot = jnp.dot(vals_t, p, preferred_element_type=F32)
                denom = ot[ATT_HD:ATT_HD + 1, :] + jnp.exp2(sink - m)
                ot = ot[:ATT_HD, :] * (1.0 / denom)
                for j in range(UNIT_HEADS):
                    head = kv * ATT_GROUP + part * UNIT_HEADS + j
                    att_ref[head * ATT_HD:(head + 1) * ATT_HD, blk * WINDOW:(blk + 1) * WINDOW] = (
                        ot[:, j * WINDOW:(j + 1) * WINDOW].astype(BF16))
    y = lax.dot_general(att_ref[...], wo_ref[...], TN_DIMS, preferred_element_type=F32)
    o_ref[0] = x_ref[0] + y


def _attention_layer(x, g, k4, vt, w_q, q_norm, sinks, w_o, *, tq=512):
    b, s, d = x.shape
    heads = d // ATT_HD
    n_slabs = heads // SLAB_HEADS
    half = ATT_HD // 2
    nblk = tq // WINDOW
    assert s % tq == 0
    wq = (w_q * g[:, None]).reshape(d, n_slabs, SLAB_HEADS, 2, half).transpose(0, 1, 3, 2, 4).reshape(d, d)
    wq = wq.astype(BF16)
    cos, sin = _rope_tables(s, ATT_HD ** -0.5 * LOG2_E)
    g1, g2 = jnp.tile(q_norm[:half], SLAB_HEADS)[None, :], jnp.tile(q_norm[half:], SLAB_HEADS)[None, :]
    rot = jnp.stack([cos * g1, sin * g2, cos * g2, sin * g1])
    lane = np.arange(MXU_TILE)
    head_of = (lane % (MXU_TILE // 2)) // half
    ind = jnp.asarray((head_of[:, None] == head_of[None, :]) / ATT_HD, BF16)
    kj = np.arange(2 * WINDOW)[:, None]
    qi = np.arange(WINDOW)[None, :]
    band = (kj > qi) & (kj <= qi + WINDOW)
    bias = jnp.asarray(np.stack([np.where(band, 0.0, NEG_INF), np.where(band & (kj >= WINDOW), 0.0, NEG_INF)]), F32)
    n_kv = heads // ATT_GROUP
    sink_rows = jnp.repeat(sinks.astype(F32).reshape(n_kv, ATT_GROUP) * LOG2_E, WINDOW, axis=1)[:, None, :]
    prev_blk = lambda j: jnp.maximum(j * nblk - 1, 0)
    return pl.pallas_call(
        _attn_body,
        grid=(b, s // tq),
        in_specs=[
            pl.BlockSpec((1, tq, d), lambda i, j: (i, j, 0)),
            pl.BlockSpec((4, tq, MXU_TILE // 2), lambda i, j: (0, j, 0)),
            _resident(ind.shape), _resident(bias.shape), _resident(sink_rows.shape),
            pl.BlockSpec((1, WINDOW, k4.shape[2]), lambda i, j: (i, prev_blk(j), 0)),
            pl.BlockSpec((1, tq, k4.shape[2]), lambda i, j: (i, j, 0)),
            pl.BlockSpec((1, vt.shape[1], WINDOW), lambda i, j: (i, 0, prev_blk(j))),
            pl.BlockSpec((1, vt.shape[1], tq), lambda i, j: (i, 0, j)),
            _resident(wq.shape), _resident(w_o.shape),
        ],
        out_specs=pl.BlockSpec((1, tq, d), lambda i, j: (i, j, 0)),
        out_shape=jax.ShapeDtypeStruct((b, s, d), F32),
        scratch_shapes=[pltpu.VMEM((tq, d), BF16),
                        pltpu.VMEM((nblk, n_kv, ATT_GROUP * WINDOW, MXU_TILE), BF16),
                        pltpu.VMEM((nblk, n_kv, ATT_GROUP // UNIT_HEADS, 2 * WINDOW, UNIT_HEADS * WINDOW), F32),
                        pltpu.VMEM((d, tq), BF16)],
        compiler_params=pltpu.CompilerParams(
            dimension_semantics=("parallel", "parallel"), vmem_limit_bytes=VMEM_LIMIT_BYTES),
        name="swa_attention",
    )(x, rot, ind, bias, sink_rows, k4, k4, vt, vt, wq, w_o.astype(BF16))


def _ffn_layer(x, g, w_in, w_out, layer):
    b, s, d = x.shape
    return _ffn(x.reshape(b * s, d), g, w_in, w_out, layer).reshape(b, s, d)


def kernel(x, ffn1_norm, ffn1_w_in, ffn1_w_out, mix_norm, ffn2_norm, ffn2_w_in, ffn2_w_out,
           ret_w_in, ret_w_out, kv_norm, kv_w, k_norm, attn_w_q, q_norm, attn_sinks, attn_w_o):
    depth = ffn1_norm.shape[0]
    n_ret = ret_w_in.shape[0]
    k4 = v4 = None
    ffn1_w_in, ffn1_w_out = ffn1_w_in.astype(BF16), ffn1_w_out.astype(BF16)
    ffn2_w_in, ffn2_w_out = ffn2_w_in.astype(BF16), ffn2_w_out.astype(BF16)
    for i in range(depth):
        x = _ffn_layer(x, ffn1_norm[i], ffn1_w_in, ffn1_w_out, i)
        if i < n_ret:
            x = _retention_layer(x, mix_norm[i], ret_w_in[i], ret_w_out[i])
        else:
            j = i - n_ret
            x = _attention_layer(x, mix_norm[i], k4, v4, attn_w_q[j], q_norm[j], attn_sinks[j], attn_w_o[j])
        x = _ffn_layer(x, ffn2_norm[i], ffn2_w_in, ffn2_w_out, i)
        if i == n_ret - 1:
            k4, v4 = _shared_kv(x, kv_norm, kv_w, k_norm)
    return x
```

```python
import functools

import jax
import jax.numpy as jnp
import numpy as np
from jax import lax
from jax.experimental import pallas as pl
from jax.experimental.pallas import tpu as pltpu

F32 = jnp.float32
BF16 = jnp.bfloat16

NORM_EPS = 1e-6
MXU_TILE = 256
VMEM_LIMIT_BYTES = 56 * 1024 * 1024


def _resident(shape):
    return pl.BlockSpec(shape, lambda *_: (0,) * len(shape), pipeline_mode=pl.Buffered(1))


def _rms_scale(x):
    return lax.rsqrt(jnp.mean(x * x, axis=-1, keepdims=True) + NORM_EPS)


FFN_NORM_PIECES = 8


def _ffn_body(x_ref, xnext_ref, win_ref, wo_ref, o_ref, h_ref, a_ref, *, ff_chunk):
    tm = x_ref.shape[0]
    d_ff = wo_ref.shape[0]
    n_chunks = d_ff // ff_chunk
    piece = tm // FFN_NORM_PIECES
    step = pl.program_id(0)
    cur, nxt = step % 2, (step + 1) % 2

    def norm_rows(src_ref, slot, rows):
        x = src_ref[rows, :]
        h_ref[slot, rows, :] = (x * _rms_scale(x)).astype(BF16)

    @pl.when(step == 0)
    def _():
        norm_rows(x_ref, 0, slice(0, tm))

    for c in range(n_chunks):
        sl = slice(c * ff_chunk, (c + 1) * ff_chunk)
        gate = jnp.dot(h_ref[cur], win_ref[:, sl], preferred_element_type=F32)
        up = jnp.dot(h_ref[cur], win_ref[:, d_ff + c * ff_chunk:d_ff + (c + 1) * ff_chunk],
                     preferred_element_type=F32)
        a_ref[:, sl] = (gate * jax.nn.sigmoid(gate) * up).astype(BF16)
        if c < FFN_NORM_PIECES:
            norm_rows(xnext_ref, nxt, slice(c * piece, (c + 1) * piece))
    y = jnp.dot(a_ref[...], wo_ref[...], preferred_element_type=F32)
    o_ref[...] = x_ref[...] + 0.5 * y


def _layer_resident(stacked, layer):
    return pl.BlockSpec((None,) + stacked.shape[1:], lambda *_: (layer, 0, 0), pipeline_mode=pl.Buffered(1))


def _ffn(x2d, w_in, wo, layer, *, tm=1024, ff_chunk=MXU_TILE):
    t, d = x2d.shape
    d_ff = wo.shape[1]
    steps = t // tm
    assert t % tm == 0 and d_ff % ff_chunk == 0 and d_ff // ff_chunk >= FFN_NORM_PIECES
    return pl.pallas_call(
        functools.partial(_ffn_body, ff_chunk=ff_chunk),
        grid=(steps,),
        in_specs=[
            pl.BlockSpec((tm, d), lambda i: (i, 0)),
            pl.BlockSpec((tm, d), lambda i: (jnp.minimum(i + 1, steps - 1), 0)),
            _layer_resident(w_in, layer),
            _layer_resident(wo, layer),
        ],
        out_specs=pl.BlockSpec((tm, d), lambda i: (i, 0)),
        out_shape=jax.ShapeDtypeStruct((t, d), F32),
        scratch_shapes=[pltpu.VMEM((2, tm, d), BF16), pltpu.VMEM((tm, d_ff), BF16)],
        compiler_params=pltpu.CompilerParams(
            dimension_semantics=("arbitrary",), vmem_limit_bytes=VMEM_LIMIT_BYTES),
        name="ffn",
    )(x2d, x2d, w_in, wo)


RET_DK = 256
RET_DV_FACTOR = 2
RET_ROT_BASE = 10000.0
NT_DIMS = (((1,), (1,)), ((), ()))
TN_DIMS = (((0,), (0,)), ((), ()))


def _ret_gamma(h):
    return 1.0 - 2.0 ** (-5.0 - h)


def _rotate_halves(t, cos, sin):
    half = t.shape[1] // 2
    te, to = t[:, :half], t[:, half:]
    return jnp.concatenate([te * cos - to * sin, to * cos + te * sin], axis=1)


def _ret_body(x_ref, rot_ref, causal_ref, wqk_ref, wvg_ref, wo_ref,
              o_ref, h_ref, gated_ref, state_ref, *, heads, dk, dv, chunk):
    rows = x_ref.shape[1]

    @pl.when(pl.program_id(1) == 0)
    def _():
        state_ref[...] = jnp.zeros_like(state_ref)

    x = x_ref[0]
    h_ref[...] = (x * _rms_scale(x)).astype(BF16)
    for hd in range(heads):
        proj = lambda w_ref, off, width: jnp.dot(
            h_ref[...], w_ref[:, off + hd * width: off + (hd + 1) * width], preferred_element_type=F32)
        qs = _rotate_halves(proj(wqk_ref, 0, dk), rot_ref[0, hd], rot_ref[1, hd]).astype(BF16)
        ks = _rotate_halves(proj(wqk_ref, heads * dk, dk), rot_ref[2, hd], rot_ref[3, hd]).astype(BF16)
        v = proj(wvg_ref, 0, dv).astype(BF16)
        gate = 0.5 * proj(wvg_ref, heads * dv, dv)
        gate = gate + gate * jnp.tanh(gate)
        state = state_ref[hd]
        for c in range(rows // chunk):
            sl = slice(c * chunk, (c + 1) * chunk)
            qc, kc, vc = qs[sl], ks[sl], v[sl]
            scores = lax.dot_general(qc, kc, NT_DIMS, preferred_element_type=F32)
            scores = (scores * causal_ref[...]).astype(BF16)
            out = (jnp.dot(scores, vc, preferred_element_type=F32)
                   + jnp.dot(qc, state.astype(BF16), preferred_element_type=F32))
            state = (state + lax.dot_general(kc, vc, TN_DIMS, preferred_element_type=F32)) * (_ret_gamma(hd) ** chunk)
            out = out * _rms_scale(out)
            gated_ref[sl, hd * dv:(hd + 1) * dv] = (out * gate[sl]).astype(BF16)
        state_ref[hd] = state
    y = jnp.dot(gated_ref[...], wo_ref[...], preferred_element_type=F32)
    o_ref[0] = x_ref[0] + y


def _ret_tables(s, chunk, heads, dk):
    half = dk // 2
    freq = 1.0 / (RET_ROT_BASE ** np.linspace(0.0, 1.0, half))
    ang = np.arange(s)[:, None] * freq[None, :]
    cos, sin = np.cos(ang)[None], np.sin(ang)[None]
    gamma = np.array([_ret_gamma(h) for h in range(heads)])
    assert gamma.min() ** -chunk < 2.0 ** 16, "chunk too long to split the decay into two factors"
    in_chunk = (np.arange(s) % chunk) + 1.0
    q_scale = (gamma[:, None] ** in_chunk[None, :])[:, :, None]
    k_scale = (gamma[:, None] ** -in_chunk[None, :])[:, :, None] * dk ** -0.5
    rot = np.stack([cos * q_scale, sin * q_scale, cos * k_scale, sin * k_scale])
    n = np.arange(chunk)
    causal = (n[:, None] >= n[None, :]).astype(np.float64)
    return jnp.asarray(rot, F32), jnp.asarray(causal, F32)


def _deinterleave_heads(w, heads, dk):
    d = w.shape[0]
    return w.reshape(d, heads, dk // 2, 2).transpose(0, 1, 3, 2).reshape(d, heads * dk)


def _retention_layer(x, g, w_in, w_out, *, chunk=MXU_TILE, ts=2 * MXU_TILE):
    b, s, d = x.shape
    heads = d // RET_DK
    dk, dv = RET_DK, RET_DV_FACTOR * d // heads
    assert s % ts == 0 and ts % chunk == 0
    dq = heads * dk
    w_in = w_in * g[:, None]
    wqk = _deinterleave_heads(w_in[:, :2 * dq], 2 * heads, dk).astype(BF16)
    wvg = w_in[:, 2 * dq:].astype(BF16)
    rot, causal = _ret_tables(s, chunk, heads, dk)
    return pl.pallas_call(
        functools.partial(_ret_body, heads=heads, dk=dk, dv=dv, chunk=chunk),
        grid=(b, s // ts),
        in_specs=[
            pl.BlockSpec((1, ts, d), lambda i, j: (i, j, 0)),
            pl.BlockSpec((4, heads, ts, dk // 2), lambda i, j: (0, 0, j, 0)),
            _resident(causal.shape),
            _resident(wqk.shape),
            _resident(wvg.shape),
            _resident(w_out.shape),
        ],
        out_specs=pl.BlockSpec((1, ts, d), lambda i, j: (i, j, 0)),
        out_shape=jax.ShapeDtypeStruct((b, s, d), F32),
        scratch_shapes=[pltpu.VMEM((ts, d), BF16), pltpu.VMEM((ts, heads * dv), BF16),
                        pltpu.VMEM((heads, dk, dv), F32)],
        compiler_params=pltpu.CompilerParams(
            dimension_semantics=("parallel", "arbitrary"), vmem_limit_bytes=VMEM_LIMIT_BYTES),
        name="retention",
    )(x, rot, causal, wqk, wvg, w_out.astype(BF16))


ATT_HD = 64
ATT_GROUP = 8
SLAB_HEADS = MXU_TILE // ATT_HD
WINDOW = 128
ROPE_THETA = 10000.0
NEG_INF = -1e30
LOG2_E = 1.4426950408889634
UNIT_HEADS = 8
ONES_ROWS = 16


def _rope_tables(s, scale):
    half = ATT_HD // 2
    inv_freq = 1.0 / (ROPE_THETA ** (np.arange(0, ATT_HD, 2) / ATT_HD))
    ang = np.arange(s)[:, None] * inv_freq[None, :]
    return (jnp.asarray(np.tile(np.cos(ang), (1, SLAB_HEADS)) * scale, F32),
            jnp.asarray(np.tile(np.sin(ang), (1, SLAB_HEADS)) * scale, F32))


def _slab_gain(g):
    half = ATT_HD // 2
    return jnp.concatenate([jnp.tile(g[:half], SLAB_HEADS), jnp.tile(g[half:], SLAB_HEADS)]).reshape(1, MXU_TILE)


def _kv_body(x_ref, g_ref, kg_ref, cos_ref, sin_ref, wk_ref, wv_ref, k_ref, vt_ref, *, kv_heads):
    x = x_ref[0]
    h = (x * _rms_scale(x) * g_ref[...]).astype(BF16)
    for g in range(kv_heads):
        sl = slice(g * MXU_TILE, (g + 1) * MXU_TILE)
        k = jnp.dot(h, wk_ref[:, sl], preferred_element_type=F32)
        k = k * _rms_scale(k) * kg_ref[...]
        k_ref[0, :, sl] = _rotate_halves(k, cos_ref[...], sin_ref[...]).astype(BF16)
    v = jnp.dot(h, wv_ref[...], preferred_element_type=F32)
    vt_ref[0] = v.T.astype(BF16)


def _shared_kv(x, kv_norm, kv_w, k_norm, *, ts=1024):
    b, s, d = x.shape
    kv_heads = kv_w.shape[1] // (2 * ATT_HD)
    half = ATT_HD // 2
    wk = kv_w[:, :kv_heads * ATT_HD].reshape(d, kv_heads, ATT_HD)
    wk = jnp.concatenate([jnp.tile(wk[:, :, :half], (1, 1, SLAB_HEADS)),
                          jnp.tile(wk[:, :, half:], (1, 1, SLAB_HEADS))], axis=2)
    wk = wk.reshape(d, kv_heads * MXU_TILE).astype(BF16)
    wv = kv_w[:, kv_heads * ATT_HD:].astype(BF16)
    cos, sin = _rope_tables(s, 1.0)
    rot_spec = pl.BlockSpec((ts, MXU_TILE // 2), lambda i, j: (j, 0))
    return pl.pallas_call(
        functools.partial(_kv_body, kv_heads=kv_heads),
        grid=(b, s // ts),
        in_specs=[pl.BlockSpec((1, ts, d), lambda i, j: (i, j, 0)), _resident((1, d)), _resident((1, MXU_TILE)),
                  rot_spec, rot_spec, _resident(wk.shape), _resident(wv.shape)],
        out_specs=[pl.BlockSpec((1, ts, kv_heads * MXU_TILE), lambda i, j: (i, j, 0)),
                   pl.BlockSpec((1, kv_heads * ATT_HD, ts), lambda i, j: (i, 0, j))],
        out_shape=[jax.ShapeDtypeStruct((b, s, kv_heads * MXU_TILE), BF16),
                   jax.ShapeDtypeStruct((b, kv_heads * ATT_HD, s), BF16)],
        compiler_params=pltpu.CompilerParams(
            dimension_semantics=("parallel", "parallel"), vmem_limit_bytes=VMEM_LIMIT_BYTES),
        name="shared_kv",
    )(x, kv_norm.reshape(1, d), _slab_gain(k_norm), cos, sin, wk, wv)


def _attn_body(x_ref, rot_ref, ind_ref, bias_ref, sink_ref, kprev_ref, kcur_ref,
               vprev_ref, vcur_ref, wq_ref, wo_ref, o_ref, h_ref, lhs_ref, st_ref, att_ref):
    tq, d = x_ref.shape[1], x_ref.shape[2]
    n_slabs = d // MXU_TILE
    slabs_per_kv = ATT_GROUP // SLAB_HEADS
    n_kv = n_slabs // slabs_per_kv
    n_blk = tq // WINDOW
    x = x_ref[0]
    h_ref[...] = (x * _rms_scale(x)).astype(BF16)
    lane = lax.broadcasted_iota(jnp.int32, (WINDOW, MXU_TILE), 1)
    q_lanes = [(lane % (MXU_TILE // 2)) // (ATT_HD // 2) == j for j in range(SLAB_HEADS)]
    first = jnp.where(pl.program_id(1) == 0, 1, 0)

    def project(slab):
        return jnp.dot(h_ref[...], wq_ref[:, slab * MXU_TILE:(slab + 1) * MXU_TILE], preferred_element_type=F32)

    half = MXU_TILE // 2
    q_next = project(0)
    for slab in range(n_slabs):
        q = q_next
        if slab + 1 < n_slabs:
            q_next = project(slab + 1)
        ms = jnp.dot((q * q).astype(BF16), ind_ref[...], preferred_element_type=F32)
        q = q * lax.rsqrt(ms + NORM_EPS)
        q1, q2 = q[:, :half], q[:, half:]
        q = jnp.concatenate([q1 * rot_ref[0] - q2 * rot_ref[1], q2 * rot_ref[2] + q1 * rot_ref[3]], axis=1)
        q = q.astype(BF16)
        kv, sub = divmod(slab, slabs_per_kv)
        for blk in range(n_blk):
            rows = q[blk * WINDOW:(blk + 1) * WINDOW]
            for j in range(SLAB_HEADS):
                r0 = (sub * SLAB_HEADS + j) * WINDOW
                lhs_ref[blk, kv, r0:r0 + WINDOW, :] = jnp.where(q_lanes[j], rows, jnp.zeros_like(rows))

    cols = UNIT_HEADS * WINDOW
    parts = ATT_GROUP // UNIT_HEADS
    for blk in range(n_blk):
        for kv in range(n_kv):
            k_sl = slice(kv * MXU_TILE, (kv + 1) * MXU_TILE)
            if blk == 0:
                keys = jnp.concatenate([kprev_ref[0, :, k_sl], kcur_ref[0, :WINDOW, k_sl]], axis=0)
            else:
                keys = kcur_ref[0, (blk - 1) * WINDOW:(blk + 1) * WINDOW, k_sl]
            for part in range(parts):
                c_sl = slice(part * cols, (part + 1) * cols)
                st_ref[blk, kv, part] = lax.dot_general(
                    keys, lhs_ref[blk, kv, c_sl, :], NT_DIMS, preferred_element_type=F32)
    for blk in range(n_blk):
        for kv in range(n_kv):
            v_sl = slice(kv * ATT_HD, (kv + 1) * ATT_HD)
            if blk == 0:
                vals_t = jnp.concatenate([vprev_ref[0, v_sl, :], vcur_ref[0, v_sl, :WINDOW]], axis=1)
                bias = jnp.tile(bias_ref[first], (1, UNIT_HEADS))
            else:
                vals_t = vcur_ref[0, v_sl, (blk - 1) * WINDOW:(blk + 1) * WINDOW]
                bias = jnp.tile(bias_ref[0], (1, UNIT_HEADS))
            vals_t = jnp.concatenate([vals_t, jnp.ones((ONES_ROWS, 2 * WINDOW), BF16)], axis=0)
            for part in range(parts):
                c_sl = slice(part * cols, (part + 1) * cols)
                st = st_ref[blk, kv, part] + bias
                sink = sink_ref[kv, :, c_sl]
                m = jnp.maximum(jnp.max(st, axis=0, keepdims=True), sink)
                p = jnp.exp2(st - m).astype(BF16)
                ot = jnp.dot(vals_t, p, preferred_element_type=F32)
                denom = ot[ATT_HD:ATT_HD + 1, :] + jnp.exp2(sink - m)
                ot = ot[:ATT_HD, :] * (1.0 / denom)
                for j in range(UNIT_HEADS):
                    head = kv * ATT_GROUP + part * UNIT_HEADS + j
                    att_ref[head * ATT_HD:(head + 1) * ATT_HD, blk * WINDOW:(blk + 1) * WINDOW] = (
                        ot[:, j * WINDOW:(j + 1) * WINDOW].astype(BF16))
    y = lax.dot_general(att_ref[...], wo_ref[...], TN_DIMS, preferred_element_type=F32)
    o_ref[0] = x_ref[0] + y


def _attention_layer(x, g, k4, vt, w_q, q_norm, sinks, w_o, *, tq=512):
    b, s, d = x.shape
    heads = d // ATT_HD
    n_slabs = heads // SLAB_HEADS
    half = ATT_HD // 2
    nblk = tq // WINDOW
    assert s % tq == 0
    wq = (w_q * g[:, None]).reshape(d, n_slabs, SLAB_HEADS, 2, half).transpose(0, 1, 3, 2, 4).reshape(d, d)
    wq = wq.astype(BF16)
    cos, sin = _rope_tables(s, ATT_HD ** -0.5 * LOG2_E)
    g1, g2 = jnp.tile(q_norm[:half], SLAB_HEADS)[None, :], jnp.tile(q_norm[half:], SLAB_HEADS)[None, :]
    rot = jnp.stack([cos * g1, sin * g2, cos * g2, sin * g1])
    lane = np.arange(MXU_TILE)
    head_of = (lane % (MXU_TILE // 2)) // half
    ind = jnp.asarray((head_of[:, None] == head_of[None, :]) / ATT_HD, BF16)
    kj = np.arange(2 * WINDOW)[:, None]
    qi = np.arange(WINDOW)[None, :]
    band = (kj > qi) & (kj <= qi + WINDOW)
    bias = jnp.asarray(np.stack([np.where(band, 0.0, NEG_INF), np.where(band & (kj >= WINDOW), 0.0, NEG_INF)]), F32)
    n_kv = heads // ATT_GROUP
    sink_rows = jnp.repeat(sinks.astype(F32).reshape(n_kv, ATT_GROUP) * LOG2_E, WINDOW, axis=1)[:, None, :]
    prev_blk = lambda j: jnp.maximum(j * nblk - 1, 0)
    return pl.pallas_call(
        _attn_body,
        grid=(b, s // tq),
        in_specs=[
            pl.BlockSpec((1, tq, d), lambda i, j: (i, j, 0)),
            pl.BlockSpec((4, tq, MXU_TILE // 2), lambda i, j: (0, j, 0)),
            _resident(ind.shape), _resident(bias.shape), _resident(sink_rows.shape),
            pl.BlockSpec((1, WINDOW, k4.shape[2]), lambda i, j: (i, prev_blk(j), 0)),
            pl.BlockSpec((1, tq, k4.shape[2]), lambda i, j: (i, j, 0)),
            pl.BlockSpec((1, vt.shape[1], WINDOW), lambda i, j: (i, 0, prev_blk(j))),
            pl.BlockSpec((1, vt.shape[1], tq), lambda i, j: (i, 0, j)),
            _resident(wq.shape), _resident(w_o.shape),
        ],
        out_specs=pl.BlockSpec((1, tq, d), lambda i, j: (i, j, 0)),
        out_shape=jax.ShapeDtypeStruct((b, s, d), F32),
        scratch_shapes=[pltpu.VMEM((tq, d), BF16),
                        pltpu.VMEM((nblk, n_kv, ATT_GROUP * WINDOW, MXU_TILE), BF16),
                        pltpu.VMEM((nblk, n_kv, ATT_GROUP // UNIT_HEADS, 2 * WINDOW, UNIT_HEADS * WINDOW), F32),
                        pltpu.VMEM((d, tq), BF16)],
        compiler_params=pltpu.CompilerParams(
            dimension_semantics=("parallel", "parallel"), vmem_limit_bytes=VMEM_LIMIT_BYTES),
        name="swa_attention",
    )(x, rot, ind, bias, sink_rows, k4, k4, vt, vt, wq, w_o.astype(BF16))


def _ffn_layer(x, w_in, w_out, layer):
    b, s, d = x.shape
    return _ffn(x.reshape(b * s, d), w_in, w_out, layer).reshape(b, s, d)


def _ffn_weights(norm, w_in, w_out):
    return (w_in * norm[:, :, None]).astype(BF16), w_out.astype(BF16)


def kernel(x, ffn1_norm, ffn1_w_in, ffn1_w_out, mix_norm, ffn2_norm, ffn2_w_in, ffn2_w_out,
           ret_w_in, ret_w_out, kv_norm, kv_w, k_norm, attn_w_q, q_norm, attn_sinks, attn_w_o):
    depth = ffn1_norm.shape[0]
    n_ret = ret_w_in.shape[0]
    k4 = v4 = None
    ffn1_w_in, ffn1_w_out = _ffn_weights(ffn1_norm, ffn1_w_in, ffn1_w_out)
    ffn2_w_in, ffn2_w_out = _ffn_weights(ffn2_norm, ffn2_w_in, ffn2_w_out)
    for i in range(depth):
        x = _ffn_layer(x, ffn1_w_in, ffn1_w_out, i)
        if i < n_ret:
            x = _retention_layer(x, mix_norm[i], ret_w_in[i], ret_w_out[i])
        else:
            j = i - n_ret
            x = _attention_layer(x, mix_norm[i], k4, v4, attn_w_q[j], q_norm[j], attn_sinks[j], attn_w_o[j])
        x = _ffn_layer(x, ffn2_w_in, ffn2_w_out, i)
        if i == n_ret - 1:
            k4, v4 = _shared_kv(x, kv_norm, kv_w, k_norm)
    return x
```

```python
import functools

import jax
import jax.numpy as jnp
import numpy as np
from jax import lax
from jax.experimental import pallas as pl
from jax.experimental.pallas import tpu as pltpu

F32 = jnp.float32
BF16 = jnp.bfloat16

NORM_EPS = 1e-6
MXU_TILE = 256
VMEM_LIMIT_BYTES = 56 * 1024 * 1024


def _resident(shape):
    return pl.BlockSpec(shape, lambda *_: (0,) * len(shape), pipeline_mode=pl.Buffered(1))


def _rms_scale(x):
    return lax.rsqrt(jnp.mean(x * x, axis=-1, keepdims=True) + NORM_EPS)


FFN_NORM_PIECES = 8


def _ffn_body(x_ref, xnext_ref, g_ref, win_ref, wo_ref, o_ref, h_ref, a_ref, *, ff_chunk):
    tm = x_ref.shape[0]
    d_ff = wo_ref.shape[0]
    n_chunks = d_ff // ff_chunk
    piece = tm // FFN_NORM_PIECES
    step = pl.program_id(0)
    cur, nxt = step % 2, (step + 1) % 2

    def norm_rows(src_ref, slot, rows):
        x = src_ref[rows, :]
        h_ref[slot, rows, :] = (x * _rms_scale(x) * g_ref[...]).astype(BF16)

    @pl.when(step == 0)
    def _():
        norm_rows(x_ref, 0, slice(0, tm))

    for c in range(n_chunks):
        sl = slice(c * ff_chunk, (c + 1) * ff_chunk)
        gate = jnp.dot(h_ref[cur], win_ref[:, sl], preferred_element_type=F32)
        up = jnp.dot(h_ref[cur], win_ref[:, d_ff + c * ff_chunk:d_ff + (c + 1) * ff_chunk],
                     preferred_element_type=F32)
        a_ref[:, sl] = (gate * jax.nn.sigmoid(gate) * up).astype(BF16)
        if c < FFN_NORM_PIECES:
            norm_rows(xnext_ref, nxt, slice(c * piece, (c + 1) * piece))
    y = jnp.dot(a_ref[...], wo_ref[...], preferred_element_type=F32)
    o_ref[...] = x_ref[...] + 0.5 * y


def _layer_resident(stacked, layer):
    return pl.BlockSpec((None,) + stacked.shape[1:], lambda *_: (layer, 0, 0), pipeline_mode=pl.Buffered(1))


def _ffn(x2d, g, w_in, wo, layer, *, tm=1024, ff_chunk=MXU_TILE):
    t, d = x2d.shape
    d_ff = wo.shape[1]
    steps = t // tm
    assert t % tm == 0 and d_ff % ff_chunk == 0 and d_ff // ff_chunk >= FFN_NORM_PIECES
    return pl.pallas_call(
        functools.partial(_ffn_body, ff_chunk=ff_chunk),
        grid=(steps,),
        in_specs=[
            pl.BlockSpec((tm, d), lambda i: (i, 0)),
            pl.BlockSpec((tm, d), lambda i: (jnp.minimum(i + 1, steps - 1), 0)),
            _resident((1, d)),
            _layer_resident(w_in, layer),
            _layer_resident(wo, layer),
        ],
        out_specs=pl.BlockSpec((tm, d), lambda i: (i, 0)),
        out_shape=jax.ShapeDtypeStruct((t, d), F32),
        scratch_shapes=[pltpu.VMEM((2, tm, d), BF16), pltpu.VMEM((tm, d_ff), BF16)],
        compiler_params=pltpu.CompilerParams(
            dimension_semantics=("arbitrary",), vmem_limit_bytes=VMEM_LIMIT_BYTES),
        name="ffn",
    )(x2d, x2d, g.reshape(1, d), w_in, wo)


RET_DK = 256
RET_DV_FACTOR = 2
RET_ROT_BASE = 10000.0
NT_DIMS = (((1,), (1,)), ((), ()))
TN_DIMS = (((0,), (0,)), ((), ()))


def _ret_gamma(h):
    return 1.0 - 2.0 ** (-5.0 - h)


def _rotate_halves(t, cos, sin):
    half = t.shape[1] // 2
    te, to = t[:, :half], t[:, half:]
    return jnp.concatenate([te * cos - to * sin, to * cos + te * sin], axis=1)


def _ret_body(x_ref, rot_ref, causal_ref, wqk_ref, wvg_ref, wo_ref,
              o_ref, h_ref, gated_ref, state_ref, *, heads, dk, dv, chunk):
    rows = x_ref.shape[1]

    @pl.when(pl.program_id(1) == 0)
    def _():
        state_ref[...] = jnp.zeros_like(state_ref)

    x = x_ref[0]
    h_ref[...] = (x * _rms_scale(x)).astype(BF16)
    for hd in range(heads):
        proj = lambda w_ref, off, width: jnp.dot(
            h_ref[...], w_ref[:, off + hd * width: off + (hd + 1) * width], preferred_element_type=F32)
        qs = _rotate_halves(proj(wqk_ref, 0, dk), rot_ref[0, hd], rot_ref[1, hd]).astype(BF16)
        ks = _rotate_halves(proj(wqk_ref, heads * dk, dk), rot_ref[2, hd], rot_ref[3, hd]).astype(BF16)
        v = proj(wvg_ref, 0, dv).astype(BF16)
        gate = 0.5 * proj(wvg_ref, heads * dv, dv)
        gate = gate + gate * jnp.tanh(gate)
        state = state_ref[hd]
        for c in range(rows // chunk):
            sl = slice(c * chunk, (c + 1) * chunk)
            qc, kc, vc = qs[sl], ks[sl], v[sl]
            scores = lax.dot_general(qc, kc, NT_DIMS, preferred_element_type=F32)
            scores = (scores * causal_ref[...]).astype(BF16)
            out = (jnp.dot(scores, vc, preferred_element_type=F32)
                   + jnp.dot(qc, state.astype(BF16), preferred_element_type=F32))
            state = (state + lax.dot_general(kc, vc, TN_DIMS, preferred_element_type=F32)) * (_ret_gamma(hd) ** chunk)
            out = out * _rms_scale(out)
            gated_ref[sl, hd * dv:(hd + 1) * dv] = (out * gate[sl]).astype(BF16)
        state_ref[hd] = state
    y = jnp.dot(gated_ref[...], wo_ref[...], preferred_element_type=F32)
    o_ref[0] = x_ref[0] + y


def _ret_tables(s, chunk, heads, dk):
    half = dk // 2
    freq = 1.0 / (RET_ROT_BASE ** np.linspace(0.0, 1.0, half))
    ang = np.arange(s)[:, None] * freq[None, :]
    cos, sin = np.cos(ang)[None], np.sin(ang)[None]
    gamma = np.array([_ret_gamma(h) for h in range(heads)])
    assert gamma.min() ** -chunk < 2.0 ** 16, "chunk too long to split the decay into two factors"
    in_chunk = (np.arange(s) % chunk) + 1.0
    q_scale = (gamma[:, None] ** in_chunk[None, :])[:, :, None]
    k_scale = (gamma[:, None] ** -in_chunk[None, :])[:, :, None] * dk ** -0.5
    rot = np.stack([cos * q_scale, sin * q_scale, cos * k_scale, sin * k_scale])
    n = np.arange(chunk)
    causal = (n[:, None] >= n[None, :]).astype(np.float64)
    return jnp.asarray(rot, F32), jnp.asarray(causal, F32)


def _deinterleave_heads(w, heads, dk):
    d = w.shape[0]
    return w.reshape(d, heads, dk // 2, 2).transpose(0, 1, 3, 2).reshape(d, heads * dk)


def _retention_layer(x, g, w_in, w_out, *, chunk=MXU_TILE, ts=2 * MXU_TILE):
    b, s, d = x.shape
    heads = d // RET_DK
    dk, dv = RET_DK, RET_DV_FACTOR * d // heads
    assert s % ts == 0 and ts % chunk == 0
    dq = heads * dk
    w_in = w_in * g[:, None]
    wqk = _deinterleave_heads(w_in[:, :2 * dq], 2 * heads, dk).astype(BF16)
    wvg = w_in[:, 2 * dq:].astype(BF16)
    rot, causal = _ret_tables(s, chunk, heads, dk)
    return pl.pallas_call(
        functools.partial(_ret_body, heads=heads, dk=dk, dv=dv, chunk=chunk),
        grid=(b, s // ts),
        in_specs=[
            pl.BlockSpec((1, ts, d), lambda i, j: (i, j, 0)),
            pl.BlockSpec((4, heads, ts, dk // 2), lambda i, j: (0, 0, j, 0)),
            _resident(causal.shape),
            _resident(wqk.shape),
            _resident(wvg.shape),
            _resident(w_out.shape),
        ],
        out_specs=pl.BlockSpec((1, ts, d), lambda i, j: (i, j, 0)),
        out_shape=jax.ShapeDtypeStruct((b, s, d), F32),
        scratch_shapes=[pltpu.VMEM((ts, d), BF16), pltpu.VMEM((ts, heads * dv), BF16),
                        pltpu.VMEM((heads, dk, dv), F32)],
        compiler_params=pltpu.CompilerParams(
            dimension_semantics=("parallel", "arbitrary"), vmem_limit_bytes=VMEM_LIMIT_BYTES),
        name="retention",
    )(x, rot, causal, wqk, wvg, w_out.astype(BF16))


ATT_HD = 64
ATT_GROUP = 8
SLAB_HEADS = MXU_TILE // ATT_HD
WINDOW = 128
ROPE_THETA = 10000.0
NEG_INF = -1e30
LOG2_E = 1.4426950408889634
UNIT_HEADS = 8
ONES_ROWS = 16


def _rope_tables(s, scale):
    inv_freq = 1.0 / (ROPE_THETA ** (np.arange(0, ATT_HD, 2) / ATT_HD))
    ang = np.arange(s)[:, None] * inv_freq[None, :]
    return (jnp.asarray(np.tile(np.cos(ang), (1, SLAB_HEADS)) * scale, F32),
            jnp.asarray(np.tile(np.sin(ang), (1, SLAB_HEADS)) * scale, F32))


def _slab_gain(g):
    half = ATT_HD // 2
    return jnp.concatenate([jnp.tile(g[:half], SLAB_HEADS), jnp.tile(g[half:], SLAB_HEADS)]).reshape(1, MXU_TILE)


def _kv_body(x_ref, g_ref, kg_ref, cos_ref, sin_ref, wk_ref, wv_ref, k_ref, vt_ref, *, kv_heads):
    x = x_ref[0]
    h = (x * _rms_scale(x) * g_ref[...]).astype(BF16)
    for g in range(kv_heads):
        sl = slice(g * MXU_TILE, (g + 1) * MXU_TILE)
        k = jnp.dot(h, wk_ref[:, sl], preferred_element_type=F32)
        k = k * _rms_scale(k) * kg_ref[...]
        k_ref[0, :, sl] = _rotate_halves(k, cos_ref[...], sin_ref[...]).astype(BF16)
    v = jnp.dot(h, wv_ref[...], preferred_element_type=F32)
    vt_ref[0] = v.T.astype(BF16)


def _shared_kv(x, kv_norm, kv_w, k_norm, *, ts=1024):
    b, s, d = x.shape
    kv_heads = kv_w.shape[1] // (2 * ATT_HD)
    half = ATT_HD // 2
    wk = kv_w[:, :kv_heads * ATT_HD].reshape(d, kv_heads, ATT_HD)
    wk = jnp.concatenate([jnp.tile(wk[:, :, :half], (1, 1, SLAB_HEADS)),
                          jnp.tile(wk[:, :, half:], (1, 1, SLAB_HEADS))], axis=2)
    wk = wk.reshape(d, kv_heads * MXU_TILE).astype(BF16)
    wv = kv_w[:, kv_heads * ATT_HD:].astype(BF16)
    cos, sin = _rope_tables(s, 1.0)
    rot_spec = pl.BlockSpec((ts, MXU_TILE // 2), lambda i, j: (j, 0))
    return pl.pallas_call(
        functools.partial(_kv_body, kv_heads=kv_heads),
        grid=(b, s // ts),
        in_specs=[pl.BlockSpec((1, ts, d), lambda i, j: (i, j, 0)), _resident((1, d)), _resident((1, MXU_TILE)),
                  rot_spec, rot_spec, _resident(wk.shape), _resident(wv.shape)],
        out_specs=[pl.BlockSpec((1, ts, kv_heads * MXU_TILE), lambda i, j: (i, j, 0)),
                   pl.BlockSpec((1, kv_heads * ATT_HD, ts), lambda i, j: (i, 0, j))],
        out_shape=[jax.ShapeDtypeStruct((b, s, kv_heads * MXU_TILE), BF16),
                   jax.ShapeDtypeStruct((b, kv_heads * ATT_HD, s), BF16)],
        compiler_params=pltpu.CompilerParams(
            dimension_semantics=("parallel", "parallel"), vmem_limit_bytes=VMEM_LIMIT_BYTES),
        name="shared_kv",
    )(x, kv_norm.reshape(1, d), _slab_gain(k_norm), cos, sin, wk, wv)


def _attn_body(x_ref, rot_ref, ind_ref, bias_ref, sink_ref, kprev_ref, kcur_ref,
               vprev_ref, vcur_ref, wq_ref, wo_ref, o_ref, h_ref, lhs_ref, st_ref, att_ref):
    tq, d = x_ref.shape[1], x_ref.shape[2]
    n_slabs = d // MXU_TILE
    slabs_per_kv = ATT_GROUP // SLAB_HEADS
    n_kv = n_slabs // slabs_per_kv
    n_blk = tq // WINDOW
    x = x_ref[0]
    h_ref[...] = (x * _rms_scale(x)).astype(BF16)
    lane = lax.broadcasted_iota(jnp.int32, (WINDOW, MXU_TILE), 1)
    q_lanes = [(lane % (MXU_TILE // 2)) // (ATT_HD // 2) == j for j in range(SLAB_HEADS)]
    first = jnp.where(pl.program_id(1) == 0, 1, 0)

    def project(slab):
        return jnp.dot(h_ref[...], wq_ref[:, slab * MXU_TILE:(slab + 1) * MXU_TILE], preferred_element_type=F32)

    half = MXU_TILE // 2
    q_next = project(0)
    for slab in range(n_slabs):
        q = q_next
        if slab + 1 < n_slabs:
            q_next = project(slab + 1)
        ms = jnp.dot((q * q).astype(BF16), ind_ref[...], preferred_element_type=F32)
        q = q * lax.rsqrt(ms + NORM_EPS)
        q1, q2 = q[:, :half], q[:, half:]
        q = jnp.concatenate([q1 * rot_ref[0] - q2 * rot_ref[1], q2 * rot_ref[2] + q1 * rot_ref[3]], axis=1)
        q = q.astype(BF16)
        kv, sub = divmod(slab, slabs_per_kv)
        for blk in range(n_blk):
            rows = q[blk * WINDOW:(blk + 1) * WINDOW]
            for j in range(SLAB_HEADS):
                r0 = (sub * SLAB_HEADS + j) * WINDOW
                lhs_ref[blk, kv, r0:r0 + WINDOW, :] = jnp.where(q_lanes[j], rows, jnp.zeros_like(rows))

    cols = UNIT_HEADS * WINDOW
    parts = ATT_GROUP // UNIT_HEADS
    for blk in range(n_blk):
        for kv in range(n_kv):
            k_sl = slice(kv * MXU_TILE, (kv + 1) * MXU_TILE)
            if blk == 0:
                keys = jnp.concatenate([kprev_ref[0, :, k_sl], kcur_ref[0, :WINDOW, k_sl]], axis=0)
            else:
                keys = kcur_ref[0, (blk - 1) * WINDOW:(blk + 1) * WINDOW, k_sl]
            for part in range(parts):
                c_sl = slice(part * cols, (part + 1) * cols)
                st_ref[blk, kv, part] = lax.dot_general(
                    keys, lhs_ref[blk, kv, c_sl, :], NT_DIMS, preferred_element_type=F32)
    for blk in range(n_blk):
        for kv in range(n_kv):
            v_sl = slice(kv * ATT_HD, (kv + 1) * ATT_HD)
            if blk == 0:
                vals_t = jnp.concatenate([vprev_ref[0, v_sl, :], vcur_ref[0, v_sl, :WINDOW]], axis=1)
                bias = jnp.tile(bias_ref[first], (1, UNIT_HEADS))
            else:
                vals_t = vcur_ref[0, v_sl, (blk - 1) * WINDOW:(blk + 1) * WINDOW]
                bias = jnp.tile(bias_ref[0], (1, UNIT_HEADS))
            vals_t = jnp.concatenate([vals_t, jnp.ones((ONES_ROWS, 2 * WINDOW), BF16)], axis=0)
            for part in range(parts):
                c_sl = slice(part * cols, (part + 1) * cols)
                st = st_ref[blk, kv, part] + bias
                sink = sink_ref[kv, :, c_sl]
                m = jnp.maximum(jnp.max(st, axis=0, keepdims=True), sink)
                p = jnp.exp2(st - m).astype(BF16)
                ot = jnp.dot(vals_t, p, preferred_element_type=F32)
                denom = ot[ATT_HD:ATT_HD + 1, :] + jnp.exp2(sink - m)
                ot = ot[:ATT_HD, :] * (1.0 / denom)
                for j in range(UNIT_HEADS):
                    head = kv * ATT_GROUP + part * UNIT_HEADS + j
                    att_ref[head * ATT_HD:(head + 1) * ATT_HD, blk * WINDOW:(blk + 1) * WINDOW] = (
                        ot[:, j * WINDOW:(j + 1) * WINDOW].astype(BF16))
    y = lax.dot_general(att_ref[...], wo_ref[...], TN_DIMS, preferred_element_type=F32)
    o_ref[0] = x_ref[0] + y


def _attention_layer(x, g, k4, vt, w_q, q_norm, sinks, w_o, *, tq=512):
    b, s, d = x.shape
    heads = d // ATT_HD
    n_slabs = heads // SLAB_HEADS
    half = ATT_HD // 2
    nblk = tq // WINDOW
    assert s % tq == 0
    wq = (w_q * g[:, None]).reshape(d, n_slabs, SLAB_HEADS, 2, half).transpose(0, 1, 3, 2, 4).reshape(d, d)
    wq = wq.astype(BF16)
    cos, sin = _rope_tables(s, ATT_HD ** -0.5 * LOG2_E)
    g1, g2 = jnp.tile(q_norm[:half], SLAB_HEADS)[None, :], jnp.tile(q_norm[half:], SLAB_HEADS)[None, :]
    rot = jnp.stack([cos * g1, sin * g2, cos * g2, sin * g1])
    lane = np.arange(MXU_TILE)
    head_of = (lane % (MXU_TILE // 2)) // half
    ind = jnp.asarray((head_of[:, None] == head_of[None, :]) / ATT_HD, BF16)
    kj = np.arange(2 * WINDOW)[:, None]
    qi = np.arange(WINDOW)[None, :]
    band = (kj > qi) & (kj <= qi + WINDOW)
    bias = jnp.asarray(np.stack([np.where(band, 0.0, NEG_INF), np.where(band & (kj >= WINDOW), 0.0, NEG_INF)]), F32)
    n_kv = heads // ATT_GROUP
    sink_rows = jnp.repeat(sinks.astype(F32).reshape(n_kv, ATT_GROUP) * LOG2_E, WINDOW, axis=1)[:, None, :]
    prev_blk = lambda j: jnp.maximum(j * nblk - 1, 0)
    return pl.pallas_call(
        _attn_body,
        grid=(b, s // tq),
        in_specs=[
            pl.BlockSpec((1, tq, d), lambda i, j: (i, j, 0)),
            pl.BlockSpec((4, tq, MXU_TILE // 2), lambda i, j: (0, j, 0)),
            _resident(ind.shape), _resident(bias.shape), _resident(sink_rows.shape),
            pl.BlockSpec((1, WINDOW, k4.shape[2]), lambda i, j: (i, prev_blk(j), 0)),
            pl.BlockSpec((1, tq, k4.shape[2]), lambda i, j: (i, j, 0)),
            pl.BlockSpec((1, vt.shape[1], WINDOW), lambda i, j: (i, 0, prev_blk(j))),
            pl.BlockSpec((1, vt.shape[1], tq), lambda i, j: (i, 0, j)),
            _resident(wq.shape), _resident(w_o.shape),
        ],
        out_specs=pl.BlockSpec((1, tq, d), lambda i, j: (i, j, 0)),
        out_shape=jax.ShapeDtypeStruct((b, s, d), F32),
        scratch_shapes=[pltpu.VMEM((tq, d), BF16),
                        pltpu.VMEM((nblk, n_kv, ATT_GROUP * WINDOW, MXU_TILE), BF16),
                        pltpu.VMEM((nblk, n_kv, ATT_GROUP // UNIT_HEADS, 2 * WINDOW, UNIT_HEADS * WINDOW), F32),
                        pltpu.VMEM((d, tq), BF16)],
        compiler_params=pltpu.CompilerParams(
            dimension_semantics=("parallel", "parallel"), vmem_limit_bytes=VMEM_LIMIT_BYTES),
        name="swa_attention",
    )(x, rot, ind, bias, sink_rows, k4, k4, vt, vt, wq, w_o.astype(BF16))


def _ffn_layer(x, g, w_in, w_out, layer):
    b, s, d = x.shape
    return _ffn(x.reshape(b * s, d), g, w_in, w_out, layer).reshape(b, s, d)


def kernel(x, ffn1_norm, ffn1_w_in, ffn1_w_out, mix_norm, ffn2_norm, ffn2_w_in, ffn2_w_out,
           ret_w_in, ret_w_out, kv_norm, kv_w, k_norm, attn_w_q, q_norm, attn_sinks, attn_w_o):
    depth = ffn1_norm.shape[0]
    n_ret = ret_w_in.shape[0]
    k4 = v4 = None
    ffn1_w_in, ffn1_w_out = ffn1_w_in.astype(BF16), ffn1_w_out.astype(BF16)
    ffn2_w_in, ffn2_w_out = ffn2_w_in.astype(BF16), ffn2_w_out.astype(BF16)
    for i in range(depth):
        x = _ffn_layer(x, ffn1_norm[i], ffn1_w_in, ffn1_w_out, i)
        if i < n_ret:
            x = _retention_layer(x, mix_norm[i], ret_w_in[i], ret_w_out[i])
        else:
            j = i - n_ret
            x = _attention_layer(x, mix_norm[i], k4, v4, attn_w_q[j], q_norm[j], attn_sinks[j], attn_w_o[j])
        x = _ffn_layer(x, ffn2_norm[i], ffn2_w_in, ffn2_w_out, i)
        if i == n_ret - 1:
            k4, v4 = _shared_kv(x, kv_norm, kv_w, k_norm)
    return x
```
